```python
import math
import jax, jax.numpy as jnp
from jax import lax
import numpy as np

D_MODEL = 2048
BATCH = 2
SEQ = 4096
DEPTH = 2

N_A_LAYERS = DEPTH // 2
N_B_LAYERS = DEPTH - N_A_LAYERS
CHUNK = 128
SGU_GROUPS = 16
SGU_GROUP_DIM = D_MODEL // SGU_GROUPS
D_FF = -(-8 * D_MODEL // (3 * 256)) * 256
HEAD_DIM = 128
DIFF_HEADS = D_MODEL // (2 * HEAD_DIM)
N_SUB = 2 * DIFF_HEADS
V_DIM = 2 * HEAD_DIM
ROPE_THETA = 10000.0
Q_BLOCK = 128
EPS = 1e-6

kernel_name = 'yoco_gmlp_diffattn_adaln'


def rms_norm(x, g):
    xf = x.astype(jnp.float32)
    y = xf * lax.rsqrt(jnp.mean(xf * xf, axis=-1, keepdims=True) + EPS)
    return (y * g.astype(jnp.float32)).astype(x.dtype)


def layer_norm(x, g, b):
    xf = x.astype(jnp.float32)
    mu = jnp.mean(xf, axis=-1, keepdims=True)
    var = jnp.mean(jnp.square(xf - mu), axis=-1, keepdims=True)
    y = (xf - mu) * lax.rsqrt(var + EPS)
    return (y * g.astype(jnp.float32) + b.astype(jnp.float32)).astype(x.dtype)


def modulate(h, shift, scale):
    return h * (1.0 + scale[:, None, :]) + shift[:, None, :]


def ada_mod(c, w, b, n):
    m = jax.nn.silu(c) @ w + b
    return jnp.split(m, n, axis=-1)


def swiglu(h, wi, wo):
    g, u = jnp.split(h @ wi, 2, axis=-1)
    return (jax.nn.silu(g) * u) @ wo


def rope_tables(positions):
    inv_freq = 1.0 / (ROPE_THETA ** (jnp.arange(0, HEAD_DIM, 2, dtype=jnp.float32) / HEAD_DIM))
    ang = positions.astype(jnp.float32)[..., None] * inv_freq
    return jnp.cos(ang), jnp.sin(ang)


def apply_rope(x, cos, sin):
    xf = x.astype(jnp.float32)
    x1, x2 = jnp.split(xf, 2, axis=-1)
    cs = cos[:, :, None, :]
    sn = sin[:, :, None, :]
    return jnp.concatenate([x1 * cs - x2 * sn, x2 * cs + x1 * sn], axis=-1).astype(x.dtype)


def gmlp_layer(x, c, ada_w, ada_b, norm1_g, in_w, in_b, ln_g, ln_b, sgu_w, sgu_b, out_w,
               norm2_g, ffn_wi, ffn_wo):
    sh1, sc1, g1, sh2, sc2, g2 = ada_mod(c, ada_w, ada_b, 6)
    bsz, seq, _ = x.shape
    h = modulate(rms_norm(x, norm1_g), sh1, sc1)
    z = jax.nn.gelu(h @ in_w + in_b, approximate=False)
    u, v = jnp.split(z, 2, axis=-1)
    v = layer_norm(v, ln_g, ln_b)
    v = v.reshape(bsz, seq // CHUNK, CHUNK, SGU_GROUPS, SGU_GROUP_DIM)
    w = sgu_w * jnp.tril(jnp.ones((CHUNK, CHUNK), sgu_w.dtype))
    sv = jnp.einsum('gts,bnsgc->bntgc', w, v) + sgu_b.T[None, None, :, :, None]
    y = (u * sv.reshape(bsz, seq, D_MODEL)) @ out_w
    x = x + g1[:, None, :] * y
    h = modulate(rms_norm(x, norm2_g), sh2, sc2)
    return x + g2[:, None, :] * swiglu(h, ffn_wi, ffn_wo)


def shared_kv(x, c, kv_ada_w, kv_ada_b, kv_norm_g, kv_w, k_norm_g, cos, sin):
    bsz, seq, _ = x.shape
    sh, sc = ada_mod(c, kv_ada_w, kv_ada_b, 2)
    h = modulate(rms_norm(x, kv_norm_g), sh, sc)
    kv = h @ kv_w
    k = kv[..., :D_MODEL].reshape(bsz, seq, N_SUB, HEAD_DIM)
    k = apply_rope(rms_norm(k, k_norm_g), cos, sin)
    v = kv[..., D_MODEL:].reshape(bsz, seq, DIFF_HEADS, V_DIM)
    return k, v


def diff_attention(q, k, v, lam):
    bsz, seq, _, _ = q.shape
    n_blk = seq // Q_BLOCK
    scale = HEAD_DIM ** -0.5
    kh = k.transpose(0, 2, 1, 3)
    vh = v.transpose(0, 2, 1, 3)
    qb = q.transpose(0, 2, 1, 3).reshape(bsz, N_SUB, n_blk, Q_BLOCK, HEAD_DIM)
    qb = qb.transpose(2, 0, 1, 3, 4)
    key_pos = jnp.arange(seq)
    neg = jnp.finfo(jnp.float32).min

    def one_block(args):
        q_blk, i = args
        s = jnp.einsum('bhqd,bhkd->bhqk', q_blk, kh).astype(jnp.float32) * scale
        q_pos = i * Q_BLOCK + jnp.arange(Q_BLOCK)
        s = jnp.where(key_pos[None, :] <= q_pos[:, None], s, neg)
        p = jax.nn.softmax(s, axis=-1).reshape(bsz, DIFF_HEADS, 2, Q_BLOCK, seq)
        p = p[:, :, 0] - lam * p[:, :, 1]
        return jnp.einsum('bhqk,bhkd->bhqd', p.astype(vh.dtype), vh)

    out = lax.map(one_block, (qb, jnp.arange(n_blk)))
    return out.transpose(1, 0, 3, 2, 4).reshape(bsz, seq, DIFF_HEADS, V_DIM)


def diff_layer(x, c, k, v, cos, sin, layer_idx, ada_w, ada_b, norm1_g, q_w, q_norm_g,
               lq1, lk1, lq2, lk2, subln_g, o_w, norm2_g, ffn_wi, ffn_wo):
    sh1, sc1, g1, sh2, sc2, g2 = ada_mod(c, ada_w, ada_b, 6)
    bsz, seq, _ = x.shape
    h = modulate(rms_norm(x, norm1_g), sh1, sc1)
    q = (h @ q_w).reshape(bsz, seq, N_SUB, HEAD_DIM)
    q = apply_rope(rms_norm(q, q_norm_g), cos, sin)
    lam_init = 0.8 - 0.6 * math.exp(-0.3 * layer_idx)
    f32 = jnp.float32
    lam = (jnp.exp(jnp.sum(lq1.astype(f32) * lk1.astype(f32)))
           - jnp.exp(jnp.sum(lq2.astype(f32) * lk2.astype(f32))) + lam_init)
    o = diff_attention(q, k, v, lam)
    o = rms_norm(o, subln_g) * (1.0 - lam_init)
    y = o.reshape(bsz, seq, D_MODEL) @ o_w
    x = x + g1[:, None, :] * y
    h = modulate(rms_norm(x, norm2_g), sh2, sc2)
    return x + g2[:, None, :] * swiglu(h, ffn_wi, ffn_wo)


def setup_inputs(seed: int = 0) -> dict:
    key = jax.random.key(seed)
    ks = iter(jax.random.split(key, 48))
    f32 = jnp.float32

    def nrm(shape, fan_in, mult=1.0):
        return jax.random.normal(next(ks), shape, f32) * (mult * fan_in ** -0.5)

    def gain(shape):
        return 1.0 + 0.05 * jax.random.normal(next(ks), shape, f32)

    def bias(shape, s=0.02):
        return s * jax.random.normal(next(ks), shape, f32)

    D, NA, NB = D_MODEL, N_A_LAYERS, N_B_LAYERS
    x = jax.random.normal(next(ks), (BATCH, SEQ, D), f32)
    c = jax.random.normal(next(ks), (BATCH, D), f32)
    offs = jax.random.randint(next(ks), (BATCH, 1), 0, 1024, dtype=jnp.int32)
    positions = jnp.arange(SEQ, dtype=jnp.int32)[None, :] + offs
    return {
        'x': x, 'c': c, 'positions': positions,
        'a_ada_w': nrm((NA, D, 6 * D), D, 0.5), 'a_ada_b': bias((NA, 6 * D)),
        'a_norm1_g': gain((NA, D)),
        'a_in_w': nrm((NA, D, 2 * D), D), 'a_in_b': bias((NA, 2 * D)),
        'a_sgu_ln_g': gain((NA, D)), 'a_sgu_ln_b': bias((NA, D)),
        'a_sgu_w': nrm((NA, SGU_GROUPS, CHUNK, CHUNK), CHUNK),
        'a_sgu_b': 1.0 + bias((NA, SGU_GROUPS, CHUNK), 0.05),
        'a_out_w': nrm((NA, D, D), D),
        'a_norm2_g': gain((NA, D)),
        'a_ffn_wi': nrm((NA, D, 2 * D_FF), D), 'a_ffn_wo': nrm((NA, D_FF, D), D_FF),
        'kv_ada_w': nrm((D, 2 * D), D, 0.5), 'kv_ada_b': bias((2 * D,)),
        'kv_norm_g': gain((D,)), 'kv_w': nrm((D, 2 * D), D), 'k_norm_g': gain((HEAD_DIM,)),
        'b_ada_w': nrm((NB, D, 6 * D), D, 0.5), 'b_ada_b': bias((NB, 6 * D)),
        'b_norm1_g': gain((NB, D)),
        'b_q_w': nrm((NB, D, D), D), 'b_q_norm_g': gain((NB, HEAD_DIM)),
        'b_lambda_q1': bias((NB, HEAD_DIM), 0.1), 'b_lambda_k1': bias((NB, HEAD_DIM), 0.1),
        'b_lambda_q2': bias((NB, HEAD_DIM), 0.1), 'b_lambda_k2': bias((NB, HEAD_DIM), 0.1),
        'b_subln_g': gain((NB, V_DIM)),
        'b_o_w': nrm((NB, D, D), D),
        'b_norm2_g': gain((NB, D)),
        'b_ffn_wi': nrm((NB, D, 2 * D_FF), D), 'b_ffn_wo': nrm((NB, D_FF, D), D_FF),
    }


def reference(x, c, positions,
              a_ada_w, a_ada_b, a_norm1_g, a_in_w, a_in_b, a_sgu_ln_g, a_sgu_ln_b,
              a_sgu_w, a_sgu_b, a_out_w, a_norm2_g, a_ffn_wi, a_ffn_wo,
              kv_ada_w, kv_ada_b, kv_norm_g, kv_w, k_norm_g,
              b_ada_w, b_ada_b, b_norm1_g, b_q_w, b_q_norm_g,
              b_lambda_q1, b_lambda_k1, b_lambda_q2, b_lambda_k2, b_subln_g, b_o_w,
              b_norm2_g, b_ffn_wi, b_ffn_wo):
    cos, sin = rope_tables(positions)
    k_sh = None
    v_sh = None
    for l in range(DEPTH):
        if l < N_A_LAYERS:
            x = gmlp_layer(x, c, a_ada_w[l], a_ada_b[l], a_norm1_g[l], a_in_w[l], a_in_b[l],
                           a_sgu_ln_g[l], a_sgu_ln_b[l], a_sgu_w[l], a_sgu_b[l], a_out_w[l],
                           a_norm2_g[l], a_ffn_wi[l], a_ffn_wo[l])
        else:
            if l == N_A_LAYERS:
                k_sh, v_sh = shared_kv(x, c, kv_ada_w, kv_ada_b, kv_norm_g, kv_w, k_norm_g,
                                       cos, sin)
            j = l - N_A_LAYERS
            x = diff_layer(x, c, k_sh, v_sh, cos, sin, l, b_ada_w[j], b_ada_b[j],
                           b_norm1_g[j], b_q_w[j], b_q_norm_g[j], b_lambda_q1[j],
                           b_lambda_k1[j], b_lambda_q2[j], b_lambda_k2[j], b_subln_g[j],
                           b_o_w[j], b_norm2_g[j], b_ffn_wi[j], b_ffn_wo[j])
    return x
```

```python
import functools
import math

import jax
import jax.numpy as jnp
import numpy as np
from jax import lax
from jax.experimental import pallas as pl
from jax.experimental.pallas import tpu as pltpu

F32 = jnp.float32
BF16 = jnp.bfloat16

EPS = 1e-6
ROPE_THETA = 10000.0
HEAD_DIM = 128
CHUNK = 128
LANES = 128
SUBLANES = 8
VMEM_LIMIT_BYTES = 58 * 1024 * 1024
MASK_VALUE = -1e30


def _cparams(n_axes):
    return pltpu.CompilerParams(
        dimension_semantics=("arbitrary",) * n_axes,
        vmem_limit_bytes=VMEM_LIMIT_BYTES,
    )


def _sigmoid(x):
    return 1.0 / (1.0 + jnp.exp(-x))


def _norm_mod(x, g, sh, sc):
    ms = jnp.mean(x * x, axis=-1, keepdims=True)
    y = x * lax.rsqrt(ms + EPS)
    return (y * g) * (1.0 + sc) + sh


def _rope_kernel(pos_ref, tab_ref, cos_ref, sin_ref):
    ang = pos_ref[...].astype(F32) * tab_ref[0:1, :]
    cos_ref[...] = jnp.cos(ang)
    sin_ref[...] = jnp.sin(ang) * tab_ref[1:2, :]


def _rope_tables(positions, tm=1024):
    t = positions.size
    inv_freq = 1.0 / (ROPE_THETA ** (jnp.arange(0, HEAD_DIM, 2, dtype=F32) / HEAD_DIM))
    half = HEAD_DIM // 2
    sign = jnp.concatenate([-jnp.ones((half,), F32), jnp.ones((half,), F32)])
    tab = jnp.stack([jnp.concatenate([inv_freq, inv_freq]), sign])
    pos = positions.reshape(t, 1)
    return pl.pallas_call(
        _rope_kernel,
        grid=(t // tm,),
        in_specs=[pl.BlockSpec((tm, 1), lambda i: (i, 0)),
                  pl.BlockSpec((2, HEAD_DIM), lambda i: (0, 0))],
        out_specs=[pl.BlockSpec((tm, HEAD_DIM), lambda i: (i, 0)),
                   pl.BlockSpec((tm, HEAD_DIM), lambda i: (i, 0))],
        out_shape=[jax.ShapeDtypeStruct((t, HEAD_DIM), F32)] * 2,
        compiler_params=_cparams(1),
        name="rope_tables",
    )(pos, tab)


def _ada_kernel(c_ref, w_ref, b_ref, o_ref, *, row_chunk):
    nb, k, _ = c_ref.shape
    tn = w_ref.shape[1]
    n_chunks = k // row_chunk

    def body(kk, accs):
        r = pl.multiple_of(kk * row_chunk, row_chunk)
        w = w_ref[pl.ds(r, row_chunk), :]
        new = []
        for b in range(nb):
            cb = c_ref[b, pl.ds(r, row_chunk), :]
            sb = cb * _sigmoid(cb)
            prod = (w * sb).reshape(row_chunk // SUBLANES, SUBLANES, tn)
            new.append(accs[b] + jnp.sum(prod, axis=0))
        return tuple(new)

    accs = lax.fori_loop(0, n_chunks, body,
                         tuple(jnp.zeros((SUBLANES, tn), F32) for _ in range(nb)))
    for b in range(nb):
        o_ref[b:b + 1, :] = jnp.sum(accs[b], axis=0, keepdims=True) + b_ref[...]


def _ada_mod(c, w, b, n, tn=1024, row_chunk=64):
    nb, k = c.shape
    nd = w.shape[1]
    out = pl.pallas_call(
        functools.partial(_ada_kernel, row_chunk=row_chunk),
        grid=(nd // tn,),
        in_specs=[pl.BlockSpec((nb, k, 1), lambda j: (0, 0, 0)),
                  pl.BlockSpec((k, tn), lambda j: (0, j)),
                  pl.BlockSpec((1, tn), lambda j: (0, j))],
        out_specs=pl.BlockSpec((nb, tn), lambda j: (0, j)),
        out_shape=jax.ShapeDtypeStruct((nb, nd), F32),
        compiler_params=_cparams(1),
        name="ada_mod",
    )(c.reshape(nb, k, 1), w, b.reshape(1, nd))
    d = nd // n
    return [out[:, i * d:(i + 1) * d].reshape(nb, 1, d) for i in range(n)]


def _gmlp_kernel(x_ref, sh_ref, sc_ref, gate_ref, ng_ref, inw_ref, inb_ref, lng_ref, lnb_ref,
                 sguw_ref, sgub_ref, outw_ref, o_ref, h_ref, z_ref, p_ref, wm_ref,
                 *, nj1, tn1, tn2):
    j = pl.program_id(1)
    tm, d = x_ref.shape
    n_groups = sguw_ref.shape[0]

    @pl.when(j == 0)
    def _():
        h_ref[...] = _norm_mod(x_ref[...], ng_ref[...], sh_ref[...], sc_ref[...]).astype(BF16)

    @pl.when(j < nj1)
    def _():
        z = jnp.dot(h_ref[...], inw_ref[...].astype(BF16), preferred_element_type=F32)
        z = z + inb_ref[...]
        z = 0.5 * z * (1.0 + lax.erf(z * np.float32(math.sqrt(0.5))))
        z_ref[:, pl.ds(pl.multiple_of(j * tn1, tn1), tn1)] = z

    @pl.when(j == nj1)
    def _():
        row = lax.broadcasted_iota(jnp.int32, (CHUNK, CHUNK), 0)
        col = lax.broadcasted_iota(jnp.int32, (CHUNK, CHUNK), 1)
        for g in range(n_groups):
            wm_ref[g] = jnp.where(col <= row, sguw_ref[g], 0.0).astype(BF16)

        def chunk_body(cidx, carry):
            r = pl.multiple_of(cidx * CHUNK, CHUNK)
            v = z_ref[pl.ds(r, CHUNK), d:]
            mu = jnp.mean(v, axis=-1, keepdims=True)
            vc = v - mu
            var = jnp.mean(vc * vc, axis=-1, keepdims=True)
            vn = (vc * lax.rsqrt(var + EPS)) * lng_ref[...] + lnb_ref[...]
            vn = vn.astype(BF16)
            for g in range(n_groups):
                lo = g * CHUNK
                sv = jnp.dot(wm_ref[g], vn[:, lo:lo + CHUNK], preferred_element_type=F32)
                sv = sv + sgub_ref[:, lo:lo + CHUNK]
                u = z_ref[pl.ds(r, CHUNK), lo:lo + CHUNK]
                p_ref[pl.ds(r, CHUNK), lo:lo + CHUNK] = (u * sv).astype(BF16)
            return carry

        lax.fori_loop(0, tm // CHUNK, chunk_body, 0)

    @pl.when(j >= nj1)
    def _():
        y = jnp.dot(p_ref[...], outw_ref[...].astype(BF16), preferred_element_type=F32)
        c0 = pl.multiple_of((j - nj1) * tn2, tn2)
        o_ref[...] = x_ref[:, pl.ds(c0, tn2)] + gate_ref[:, pl.ds(c0, tn2)] * y


def _gmlp_mix(x, sh, sc, gate, norm_g, in_w, in_b, ln_g, ln_b, sgu_w, sgu_b, out_w, seq,
              tm=512, tn1=512, tn2=512):
    t, d = x.shape
    nj1 = (2 * d) // tn1
    nj2 = d // tn2
    tpb = seq // tm
    n_groups = sgu_w.shape[0]
    sgub_full = jnp.repeat(sgu_b.T, d // n_groups, axis=1)
    vec = lambda: pl.BlockSpec((None, 1, d), lambda i, j: (i // tpb, 0, 0))
    row = lambda n: pl.BlockSpec((1, n), lambda i, j: (0, 0))
    return pl.pallas_call(
        functools.partial(_gmlp_kernel, nj1=nj1, tn1=tn1, tn2=tn2),
        grid=(t // tm, nj1 + nj2),
        in_specs=[
            pl.BlockSpec((tm, d), lambda i, j: (i, 0)),
            vec(), vec(), vec(), row(d),
            pl.BlockSpec((d, tn1), lambda i, j: (0, jnp.minimum(j, nj1 - 1))),
            pl.BlockSpec((1, tn1), lambda i, j: (0, jnp.minimum(j, nj1 - 1))),
            row(d), row(d),
            pl.BlockSpec((n_groups, CHUNK, CHUNK), lambda i, j: (0, 0, 0)),
            pl.BlockSpec((CHUNK, d), lambda i, j: (0, 0)),
            pl.BlockSpec((d, tn2), lambda i, j: (0, jnp.maximum(j - nj1, 0))),
        ],
        out_specs=pl.BlockSpec((tm, tn2), lambda i, j: (i, jnp.maximum(j - nj1, 0))),
        out_shape=jax.ShapeDtypeStruct((t, d), F32),
        scratch_shapes=[pltpu.VMEM((tm, d), BF16),
                        pltpu.VMEM((tm, 2 * d), F32),
                        pltpu.VMEM((tm, d), BF16),
                        pltpu.VMEM((n_groups, CHUNK, CHUNK), BF16)],
        compiler_params=_cparams(2),
        name="gmlp_mix",
    )(x, sh, sc, gate, norm_g.reshape(1, d), in_w, in_b.reshape(1, 2 * d),
      ln_g.reshape(1, d), ln_b.reshape(1, d), sgu_w, sgub_full, out_w)


def _ffn_kernel(x_ref, sh_ref, sc_ref, gate_ref, ng_ref, wg_ref, wu_ref, wo_ref, o_ref, h_ref):
    j = pl.program_id(1)

    @pl.when(j == 0)
    def _():
        h_ref[...] = _norm_mod(x_ref[...], ng_ref[...], sh_ref[...], sc_ref[...]).astype(BF16)

    h = h_ref[...]
    g = jnp.dot(h, wg_ref[...].astype(BF16), preferred_element_type=F32)
    u = jnp.dot(h, wu_ref[...].astype(BF16), preferred_element_type=F32)
    a = (g * _sigmoid(g) * u).astype(BF16)
    y = jnp.dot(a, wo_ref[...].astype(BF16), preferred_element_type=F32)

    @pl.when(j == 0)
    def _():
        o_ref[...] = y

    @pl.when(j > 0)
    def _():
        o_ref[...] += y

    @pl.when(j == pl.num_programs(1) - 1)
    def _():
        o_ref[...] = x_ref[...] + gate_ref[...] * o_ref[...]


def _ffn(x, sh, sc, gate, norm_g, wi, wo, seq, tm=1024, tf=256):
    t, d = x.shape
    f = wo.shape[0]
    nf = f // tf
    tpb = seq // tm
    vec = lambda: pl.BlockSpec((None, 1, d), lambda i, j: (i // tpb, 0, 0))
    return pl.pallas_call(
        _ffn_kernel,
        grid=(t // tm, nf),
        in_specs=[
            pl.BlockSpec((tm, d), lambda i, j: (i, 0), pipeline_mode=pl.Buffered(1)),
            vec(), vec(), vec(),
            pl.BlockSpec((1, d), lambda i, j: (0, 0)),
            pl.BlockSpec((d, tf), lambda i, j: (0, j)),
            pl.BlockSpec((d, tf), lambda i, j: (0, j + nf)),
            pl.BlockSpec((tf, d), lambda i, j: (j, 0)),
        ],
        out_specs=pl.BlockSpec((tm, d), lambda i, j: (i, 0)),
        out_shape=jax.ShapeDtypeStruct((t, d), F32),
        scratch_shapes=[pltpu.VMEM((tm, d), BF16)],
        compiler_params=_cparams(2),
        name="swiglu_ffn",
    )(x, sh, sc, gate, norm_g.reshape(1, d), wi, wi, wo)


def _proj_kernel(x_ref, sh_ref, sc_ref, ng_ref, w_ref, hg_ref, cos_ref, sin_ref, o_ref, h_ref,
                 *, n_rope, out_scale):
    j = pl.program_id(1)
    tn = w_ref.shape[1]

    @pl.when(j == 0)
    def _():
        h_ref[...] = _norm_mod(x_ref[...], ng_ref[...], sh_ref[...], sc_ref[...]).astype(BF16)

    y = jnp.dot(h_ref[...], w_ref[...].astype(BF16), preferred_element_type=F32)

    @pl.when(j < n_rope)
    def _():
        cos = cos_ref[...]
        sin = sin_ref[...]
        hg = hg_ref[...]
        for hd in range(tn // HEAD_DIM):
            lo = hd * HEAD_DIM
            yh = y[:, lo:lo + HEAD_DIM]
            ms = jnp.mean(yh * yh, axis=-1, keepdims=True)
            yn = (yh * lax.rsqrt(ms + EPS)) * hg
            r = yn * cos + pltpu.roll(yn, HEAD_DIM // 2, axis=1) * sin
            if out_scale != 1.0:
                r = r * np.float32(out_scale)
            o_ref[:, lo:lo + HEAD_DIM] = r.astype(o_ref.dtype)

    @pl.when(j >= n_rope)
    def _():
        o_ref[...] = y.astype(o_ref.dtype)


def _proj(x, sh, sc, norm_g, w, head_g, cos2, sin2, seq, rope_cols, out_scale=1.0,
          tm=1024, tn=512):
    t, d = x.shape
    n = w.shape[1]
    tpb = seq // tm
    vec = lambda: pl.BlockSpec((None, 1, d), lambda i, j: (i // tpb, 0, 0))
    return pl.pallas_call(
        functools.partial(_proj_kernel, n_rope=rope_cols // tn, out_scale=out_scale),
        grid=(t // tm, n // tn),
        in_specs=[
            pl.BlockSpec((tm, d), lambda i, j: (i, 0)),
            vec(), vec(),
            pl.BlockSpec((1, d), lambda i, j: (0, 0)),
            pl.BlockSpec((d, tn), lambda i, j: (0, j)),
            pl.BlockSpec((1, HEAD_DIM), lambda i, j: (0, 0)),
            pl.BlockSpec((tm, HEAD_DIM), lambda i, j: (i, 0)),
            pl.BlockSpec((tm, HEAD_DIM), lambda i, j: (i, 0)),
        ],
        out_specs=pl.BlockSpec((tm, tn), lambda i, j: (i, j)),
        out_shape=jax.ShapeDtypeStruct((t, n), BF16),
        scratch_shapes=[pltpu.VMEM((tm, d), BF16)],
        compiler_params=_cparams(2),
        name="norm_proj",
    )(x, sh, sc, norm_g.reshape(1, d), w, head_g.reshape(1, HEAD_DIM), cos2, sin2)


def _attn_kernel(q_ref, k_ref, v_ref, lam_ref, sg_ref, o_ref,
                 m1_ref, l1_ref, a1_ref, m2_ref, l2_ref, a2_ref, *, lam_init, tq):
    i = pl.program_id(2)
    hd = HEAD_DIM
    q = q_ref[...]
    q1 = q[:, :hd]
    q2 = q[:, hd:]
    nt = (((1,), (1,)), ((), ()))

    for m_ref, l_ref, a_ref in ((m1_ref, l1_ref, a1_ref), (m2_ref, l2_ref, a2_ref)):
        m_ref[...] = jnp.full(m_ref.shape, MASK_VALUE, F32)
        l_ref[...] = jnp.zeros(l_ref.shape, F32)
        a_ref[...] = jnp.zeros(a_ref.shape, F32)

    def block(jb, masked):
        r = pl.multiple_of(jb * tq, tq)
        kb = k_ref[pl.ds(r, tq), :]
        vb = v_ref[pl.ds(r, tq), :]
        if masked:
            row = lax.broadcasted_iota(jnp.int32, (tq, tq), 0)
            col = lax.broadcasted_iota(jnp.int32, (tq, tq), 1)
            keep = col <= row
        for qh, lo, m_ref, l_ref, a_ref in ((q1, 0, m1_ref, l1_ref, a1_ref),
                                            (q2, hd, m2_ref, l2_ref, a2_ref)):
            s = lax.dot_general(qh, kb[:, lo:lo + hd], nt, preferred_element_type=F32)
            if masked:
                s = jnp.where(keep, s, MASK_VALUE)
            m_old = m_ref[...]
            m_new = jnp.maximum(m_old, jnp.max(s, axis=-1, keepdims=True))
            alpha = jnp.exp(m_old - m_new)
            p = jnp.exp(s - m_new)
            l_ref[...] = alpha * l_ref[...] + jnp.sum(p, axis=-1, keepdims=True)
            a_ref[...] = alpha * a_ref[...] + jnp.dot(p.astype(BF16), vb,
                                                      preferred_element_type=F32)
            m_ref[...] = m_new

    def loop_body(jb, carry):
        block(jb, False)
        return carry

    lax.fori_loop(0, i, loop_body, 0)
    block(i, True)

    lp = lam_ref[...]
    lam = (jnp.exp(jnp.sum(lp[0:1] * lp[1:2], axis=-1, keepdims=True))
           - jnp.exp(jnp.sum(lp[2:3] * lp[3:4], axis=-1, keepdims=True)) + np.float32(lam_init))
    o = a1_ref[...] / l1_ref[...] - lam * (a2_ref[...] / l2_ref[...])
    ms = jnp.mean(o * o, axis=-1, keepdims=True)
    o = (o * lax.rsqrt(ms + EPS)) * sg_ref[...] * np.float32(1.0 - lam_init)
    o_ref[...] = o.astype(o_ref.dtype)


def _diff_attention(q, kv, lam_params, subln_g, lam_init, n_heads, tq=256):
    b, s, d = q.shape
    vd = 2 * HEAD_DIM
    return pl.pallas_call(
        functools.partial(_attn_kernel, lam_init=lam_init, tq=tq),
        grid=(b, n_heads, s // tq),
        in_specs=[
            pl.BlockSpec((None, tq, vd), lambda bi, h, i: (bi, i, h)),
            pl.BlockSpec((None, s, vd), lambda bi, h, i: (bi, 0, h)),
            pl.BlockSpec((None, s, vd), lambda bi, h, i: (bi, 0, n_heads + h)),
            pl.BlockSpec((4, HEAD_DIM), lambda bi, h, i: (0, 0)),
            pl.BlockSpec((1, vd), lambda bi, h, i: (0, 0)),
        ],
        out_specs=pl.BlockSpec((None, tq, vd), lambda bi, h, i: (bi, i, h)),
        out_shape=jax.ShapeDtypeStruct((b, s, d), BF16),
        scratch_shapes=[pltpu.VMEM((tq, 1), F32), pltpu.VMEM((tq, 1), F32),
                        pltpu.VMEM((tq, vd), F32),
                        pltpu.VMEM((tq, 1), F32), pltpu.VMEM((tq, 1), F32),
                        pltpu.VMEM((tq, vd), F32)],
        compiler_params=_cparams(3),
        name="diff_attention",
    )(q, kv, kv, lam_params, subln_g.reshape(1, vd))


def _oproj_kernel(a_ref, w_ref, x_ref, gate_ref, o_ref):
    y = jnp.dot(a_ref[...], w_ref[...].astype(BF16), preferred_element_type=F32)
    o_ref[...] = x_ref[...] + gate_ref[...] * y


def _oproj(a, w, x, gate, seq, tm=1024, tn=512):
    t, d = x.shape
    k = a.shape[1]
    tpb = seq // tm
    return pl.pallas_call(
        _oproj_kernel,
        grid=(t // tm, d // tn),
        in_specs=[
            pl.BlockSpec((tm, k), lambda i, j: (i, 0)),
            pl.BlockSpec((k, tn), lambda i, j: (0, j)),
            pl.BlockSpec((tm, tn), lambda i, j: (i, j)),
            pl.BlockSpec((None, 1, tn), lambda i, j: (i // tpb, 0, j)),
        ],
        out_specs=pl.BlockSpec((tm, tn), lambda i, j: (i, j)),
        out_shape=jax.ShapeDtypeStruct((t, d), F32),
        compiler_params=_cparams(2),
        name="out_proj",
    )(a, w, x, gate)


def kernel(x, c, positions, a_ada_w, a_ada_b, a_norm1_g, a_in_w, a_in_b, a_sgu_ln_g, a_sgu_ln_b, a_sgu_w, a_sgu_b, a_out_w, a_norm2_g, a_ffn_wi, a_ffn_wo, kv_ada_w, kv_ada_b, kv_norm_g, kv_w, k_norm_g, b_ada_w, b_ada_b, b_norm1_g, b_q_w, b_q_norm_g, b_lambda_q1, b_lambda_k1, b_lambda_q2, b_lambda_k2, b_subln_g, b_o_w, b_norm2_g, b_ffn_wi, b_ffn_wo):
    bsz, seq, d = x.shape
    n_a = a_ada_w.shape[0]
    n_b = b_ada_w.shape[0]
    n_heads = d // (2 * HEAD_DIM)
    t = bsz * seq

    cos2, sin2 = _rope_tables(positions)
    xf = x.reshape(t, d)

    for l in range(n_a):
        sh1, sc1, g1, sh2, sc2, g2 = _ada_mod(c, a_ada_w[l], a_ada_b[l], 6)
        xf = _gmlp_mix(xf, sh1, sc1, g1, a_norm1_g[l], a_in_w[l], a_in_b[l], a_sgu_ln_g[l],
                       a_sgu_ln_b[l], a_sgu_w[l], a_sgu_b[l], a_out_w[l], seq)
        xf = _ffn(xf, sh2, sc2, g2, a_norm2_g[l], a_ffn_wi[l], a_ffn_wo[l], seq)

    if n_b > 0:
        sh, sc = _ada_mod(c, kv_ada_w, kv_ada_b, 2)
        kv = _proj(xf, sh, sc, kv_norm_g, kv_w, k_norm_g, cos2, sin2, seq, rope_cols=d)
        kv = kv.reshape(bsz, seq, 2 * d)

    for jl in range(n_b):
        layer_idx = n_a + jl
        lam_init = 0.8 - 0.6 * math.exp(-0.3 * layer_idx)
        sh1, sc1, g1, sh2, sc2, g2 = _ada_mod(c, b_ada_w[jl], b_ada_b[jl], 6)
        q = _proj(xf, sh1, sc1, b_norm1_g[jl], b_q_w[jl], b_q_norm_g[jl], cos2, sin2, seq,
                  rope_cols=d, out_scale=HEAD_DIM ** -0.5)
        lam_params = jnp.stack([b_lambda_q1[jl], b_lambda_k1[jl],
                                b_lambda_q2[jl], b_lambda_k2[jl]])
        o = _diff_attention(q.reshape(bsz, seq, d), kv, lam_params, b_subln_g[jl], lam_init,
                            n_heads)
        xf = _oproj(o.reshape(t, d), b_o_w[jl], xf, g1, seq)
        xf = _ffn(xf, sh2, sc2, g2, b_norm2_g[jl], b_ffn_wi[jl], b_ffn_wo[jl], seq)

    return xf.reshape(bsz, seq, d)
```

```python
import functools
import math

import jax
import jax.numpy as jnp
import numpy as np
from jax import lax
from jax.experimental import pallas as pl
from jax.experimental.pallas import tpu as pltpu

F32 = jnp.float32
BF16 = jnp.bfloat16

EPS = 1e-6
ROPE_THETA = 10000.0
HEAD_DIM = 128
CHUNK = 128
LANES = 128
SUBLANES = 8
VMEM_LIMIT_BYTES = 58 * 1024 * 1024
MASK_VALUE = -1e30


def _cparams(n_axes):
    return pltpu.CompilerParams(
        dimension_semantics=("arbitrary",) * n_axes,
        vmem_limit_bytes=VMEM_LIMIT_BYTES,
    )


def _sigmoid(x):
    return 1.0 / (1.0 + jnp.exp(-x))


def _norm_mod(x, g, sh, sc):
    ms = jnp.mean(x * x, axis=-1, keepdims=True)
    y = x * lax.rsqrt(ms + EPS)
    return (y * g) * (1.0 + sc) + sh


def _rope_kernel(pos_ref, tab_ref, cos_ref, sin_ref):
    ang = pos_ref[...].astype(F32) * tab_ref[0:1, :]
    cos_ref[...] = jnp.cos(ang)
    sin_ref[...] = jnp.sin(ang) * tab_ref[1:2, :]


def _rope_tables(positions, tm=1024):
    t = positions.size
    inv_freq = 1.0 / (ROPE_THETA ** (jnp.arange(0, HEAD_DIM, 2, dtype=F32) / HEAD_DIM))
    half = HEAD_DIM // 2
    sign = jnp.concatenate([-jnp.ones((half,), F32), jnp.ones((half,), F32)])
    tab = jnp.stack([jnp.concatenate([inv_freq, inv_freq]), sign])
    pos = positions.reshape(t, 1)
    return pl.pallas_call(
        _rope_kernel,
        grid=(t // tm,),
        in_specs=[pl.BlockSpec((tm, 1), lambda i: (i, 0)),
                  pl.BlockSpec((2, HEAD_DIM), lambda i: (0, 0))],
        out_specs=[pl.BlockSpec((tm, HEAD_DIM), lambda i: (i, 0)),
                   pl.BlockSpec((tm, HEAD_DIM), lambda i: (i, 0))],
        out_shape=[jax.ShapeDtypeStruct((t, HEAD_DIM), F32)] * 2,
        compiler_params=_cparams(1),
        name="rope_tables",
    )(pos, tab)


def _ada_kernel(c_ref, w_ref, b_ref, o_ref, *, row_chunk):
    nb, k, _ = c_ref.shape
    tn = w_ref.shape[1]
    n_chunks = k // row_chunk

    def body(kk, accs):
        r = pl.multiple_of(kk * row_chunk, row_chunk)
        w = w_ref[pl.ds(r, row_chunk), :]
        new = []
        for b in range(nb):
            cb = c_ref[b, pl.ds(r, row_chunk), :]
            sb = cb * _sigmoid(cb)
            prod = (w * sb).reshape(row_chunk // SUBLANES, SUBLANES, tn)
            new.append(accs[b] + jnp.sum(prod, axis=0))
        return tuple(new)

    accs = lax.fori_loop(0, n_chunks, body,
                         tuple(jnp.zeros((SUBLANES, tn), F32) for _ in range(nb)))
    for b in range(nb):
        o_ref[b:b + 1, :] = jnp.sum(accs[b], axis=0, keepdims=True) + b_ref[...]


def _ada_mod(c, w, b, n, tn=1024, row_chunk=64):
    nb, k = c.shape
    nd = w.shape[1]
    out = pl.pallas_call(
        functools.partial(_ada_kernel, row_chunk=row_chunk),
        grid=(nd // tn,),
        in_specs=[pl.BlockSpec((nb, k, 1), lambda j: (0, 0, 0)),
                  pl.BlockSpec((k, tn), lambda j: (0, j)),
                  pl.BlockSpec((1, tn), lambda j: (0, j))],
        out_specs=pl.BlockSpec((nb, tn), lambda j: (0, j)),
        out_shape=jax.ShapeDtypeStruct((nb, nd), F32),
        compiler_params=_cparams(1),
        name="ada_mod",
    )(c.reshape(nb, k, 1), w, b.reshape(1, nd))
    d = nd // n
    return [out[:, i * d:(i + 1) * d].reshape(nb, 1, d) for i in range(n)]


def _gmlp_kernel(x_ref, sh_ref, sc_ref, gate_ref, ng_ref, inw_ref, inb_ref, lng_ref, lnb_ref,
                 sguw_ref, sgub_ref, outw_ref, o_ref, h_ref, z_ref, p_ref, wm_ref,
                 *, nj1, tn1, tn2):
    j = pl.program_id(1)
    tm, d = x_ref.shape
    n_groups = sguw_ref.shape[0]

    @pl.when(j == 0)
    def _():
        h_ref[...] = _norm_mod(x_ref[...], ng_ref[...], sh_ref[...], sc_ref[...]).astype(BF16)

    @pl.when(j < nj1)
    def _():
        z = jnp.dot(h_ref[...], inw_ref[...].astype(BF16), preferred_element_type=F32)
        z = z + inb_ref[...]
        z = 0.5 * z * (1.0 + lax.erf(z * np.float32(math.sqrt(0.5))))
        z_ref[:, pl.ds(pl.multiple_of(j * tn1, tn1), tn1)] = z

    @pl.when(j == nj1)
    def _():
        row = lax.broadcasted_iota(jnp.int32, (CHUNK, CHUNK), 0)
        col = lax.broadcasted_iota(jnp.int32, (CHUNK, CHUNK), 1)
        for g in range(n_groups):
            wm_ref[g] = jnp.where(col <= row, sguw_ref[g], 0.0).astype(BF16)

        def chunk_body(cidx, carry):
            r = pl.multiple_of(cidx * CHUNK, CHUNK)
            v = z_ref[pl.ds(r, CHUNK), d:]
            mu = jnp.mean(v, axis=-1, keepdims=True)
            vc = v - mu
            var = jnp.mean(vc * vc, axis=-1, keepdims=True)
            vn = (vc * lax.rsqrt(var + EPS)) * lng_ref[...] + lnb_ref[...]
            vn = vn.astype(BF16)
            for g in range(n_groups):
                lo = g * CHUNK
                sv = jnp.dot(wm_ref[g], vn[:, lo:lo + CHUNK], preferred_element_type=F32)
                sv = sv + sgub_ref[:, lo:lo + CHUNK]
                u = z_ref[pl.ds(r, CHUNK), lo:lo + CHUNK]
                p_ref[pl.ds(r, CHUNK), lo:lo + CHUNK] = (u * sv).astype(BF16)
            return carry

        lax.fori_loop(0, tm // CHUNK, chunk_body, 0)

    @pl.when(j >= nj1)
    def _():
        y = jnp.dot(p_ref[...], outw_ref[...].astype(BF16), preferred_element_type=F32)
        c0 = pl.multiple_of((j - nj1) * tn2, tn2)
        o_ref[...] = x_ref[:, pl.ds(c0, tn2)] + gate_ref[:, pl.ds(c0, tn2)] * y


def _gmlp_mix(x, sh, sc, gate, norm_g, in_w, in_b, ln_g, ln_b, sgu_w, sgu_b, out_w, seq,
              tm=512, tn1=512, tn2=512):
    t, d = x.shape
    nj1 = (2 * d) // tn1
    nj2 = d // tn2
    tpb = seq // tm
    n_groups = sgu_w.shape[0]
    sgub_full = jnp.repeat(sgu_b.T, d // n_groups, axis=1)
    vec = lambda: pl.BlockSpec((None, 1, d), lambda i, j: (i // tpb, 0, 0))
    row = lambda n: pl.BlockSpec((1, n), lambda i, j: (0, 0))
    return pl.pallas_call(
        functools.partial(_gmlp_kernel, nj1=nj1, tn1=tn1, tn2=tn2),
        grid=(t // tm, nj1 + nj2),
        in_specs=[
            pl.BlockSpec((tm, d), lambda i, j: (i, 0)),
            vec(), vec(), vec(), row(d),
            pl.BlockSpec((d, tn1), lambda i, j: (0, jnp.minimum(j, nj1 - 1))),
            pl.BlockSpec((1, tn1), lambda i, j: (0, jnp.minimum(j, nj1 - 1))),
            row(d), row(d),
            pl.BlockSpec((n_groups, CHUNK, CHUNK), lambda i, j: (0, 0, 0)),
            pl.BlockSpec((CHUNK, d), lambda i, j: (0, 0)),
            pl.BlockSpec((d, tn2), lambda i, j: (0, jnp.maximum(j - nj1, 0))),
        ],
        out_specs=pl.BlockSpec((tm, tn2), lambda i, j: (i, jnp.maximum(j - nj1, 0))),
        out_shape=jax.ShapeDtypeStruct((t, d), F32),
        scratch_shapes=[pltpu.VMEM((tm, d), BF16),
                        pltpu.VMEM((tm, 2 * d), F32),
                        pltpu.VMEM((tm, d), BF16),
                        pltpu.VMEM((n_groups, CHUNK, CHUNK), BF16)],
        compiler_params=_cparams(2),
        name="gmlp_mix",
    )(x, sh, sc, gate, norm_g.reshape(1, d), in_w, in_b.reshape(1, 2 * d),
      ln_g.reshape(1, d), ln_b.reshape(1, d), sgu_w, sgub_full, out_w)


def _ffn_kernel(x_ref, sh_ref, sc_ref, gate_ref, ng_ref, wg_ref, wu_ref, wo_ref, o_ref, h_ref):
    j = pl.program_id(1)

    @pl.when(j == 0)
    def _():
        h_ref[...] = _norm_mod(x_ref[...], ng_ref[...], sh_ref[...], sc_ref[...]).astype(BF16)

    h = h_ref[...]
    g = jnp.dot(h, wg_ref[...].astype(BF16), preferred_element_type=F32)
    u = jnp.dot(h, wu_ref[...].astype(BF16), preferred_element_type=F32)
    a = (g * _sigmoid(g) * u).astype(BF16)
    y = jnp.dot(a, wo_ref[...].astype(BF16), preferred_element_type=F32)

    @pl.when(j == 0)
    def _():
        o_ref[...] = y

    @pl.when(j > 0)
    def _():
        o_ref[...] += y

    @pl.when(j == pl.num_programs(1) - 1)
    def _():
        o_ref[...] = x_ref[...] + gate_ref[...] * o_ref[...]


def _ffn(x, sh, sc, gate, norm_g, wi, wo, seq, tm=1024, tf=256):
    t, d = x.shape
    f = wo.shape[0]
    nf = f // tf
    tpb = seq // tm
    vec = lambda: pl.BlockSpec((None, 1, d), lambda i, j: (i // tpb, 0, 0))
    return pl.pallas_call(
        _ffn_kernel,
        grid=(t // tm, nf),
        in_specs=[
            pl.BlockSpec((tm, d), lambda i, j: (i, 0), pipeline_mode=pl.Buffered(1)),
            vec(), vec(), vec(),
            pl.BlockSpec((1, d), lambda i, j: (0, 0)),
            pl.BlockSpec((d, tf), lambda i, j: (0, j)),
            pl.BlockSpec((d, tf), lambda i, j: (0, j + nf)),
            pl.BlockSpec((tf, d), lambda i, j: (j, 0)),
        ],
        out_specs=pl.BlockSpec((tm, d), lambda i, j: (i, 0)),
        out_shape=jax.ShapeDtypeStruct((t, d), F32),
        scratch_shapes=[pltpu.VMEM((tm, d), BF16)],
        compiler_params=_cparams(2),
        name="swiglu_ffn",
    )(x, sh, sc, gate, norm_g.reshape(1, d), wi, wi, wo)


def _proj_kernel(x_ref, sh_ref, sc_ref, ng_ref, w_ref, hg_ref, cos_ref, sin_ref, o_ref, h_ref,
                 *, n_rope, out_scale):
    j = pl.program_id(1)
    tn = w_ref.shape[1]

    @pl.when(j == 0)
    def _():
        h_ref[...] = _norm_mod(x_ref[...], ng_ref[...], sh_ref[...], sc_ref[...]).astype(BF16)

    y = jnp.dot(h_ref[...], w_ref[...].astype(BF16), preferred_element_type=F32)

    @pl.when(j < n_rope)
    def _():
        cos = cos_ref[...]
        sin = sin_ref[...]
        hg = hg_ref[...]
        for hd in range(tn // HEAD_DIM):
            lo = hd * HEAD_DIM
            yh = y[:, lo:lo + HEAD_DIM]
            ms = jnp.mean(yh * yh, axis=-1, keepdims=True)
            yn = (yh * lax.rsqrt(ms + EPS)) * hg
            r = yn * cos + pltpu.roll(yn, HEAD_DIM // 2, axis=1) * sin
            if out_scale != 1.0:
                r = r * np.float32(out_scale)
            o_ref[:, lo:lo + HEAD_DIM] = r.astype(o_ref.dtype)

    @pl.when(j >= n_rope)
    def _():
        o_ref[...] = y.astype(o_ref.dtype)


def _proj(x, sh, sc, norm_g, w, head_g, cos2, sin2, seq, rope_cols, out_scale=1.0,
          tm=1024, tn=512):
    t, d = x.shape
    n = w.shape[1]
    tpb = seq // tm
    vec = lambda: pl.BlockSpec((None, 1, d), lambda i, j: (i // tpb, 0, 0))
    return pl.pallas_call(
        functools.partial(_proj_kernel, n_rope=rope_cols // tn, out_scale=out_scale),
        grid=(t // tm, n // tn),
        in_specs=[
            pl.BlockSpec((tm, d), lambda i, j: (i, 0)),
            vec(), vec(),
            pl.BlockSpec((1, d), lambda i, j: (0, 0)),
            pl.BlockSpec((d, tn), lambda i, j: (0, j)),
            pl.BlockSpec((1, HEAD_DIM), lambda i, j: (0, 0)),
            pl.BlockSpec((tm, HEAD_DIM), lambda i, j: (i, 0)),
            pl.BlockSpec((tm, HEAD_DIM), lambda i, j: (i, 0)),
        ],
        out_specs=pl.BlockSpec((tm, tn), lambda i, j: (i, j)),
        out_shape=jax.ShapeDtypeStruct((t, n), BF16),
        scratch_shapes=[pltpu.VMEM((tm, d), BF16)],
        compiler_params=_cparams(2),
        name="norm_proj",
    )(x, sh, sc, norm_g.reshape(1, d), w, head_g.reshape(1, HEAD_DIM), cos2, sin2)


def _attn_kernel(q_ref, k_ref, v_ref, lam_ref, sg_ref, o_ref, m_ref, l_ref, a_ref,
                 *, lam_init, tq):
    i = pl.program_id(2)
    hd = HEAD_DIM
    vd = a_ref.shape[-1]
    nt = (((1,), (1,)), ((), ()))

    m_ref[...] = jnp.full(m_ref.shape, MASK_VALUE, F32)
    l_ref[...] = jnp.zeros(l_ref.shape, F32)
    a_ref[...] = jnp.zeros(a_ref.shape, F32)

    def block(jb, masked):
        r = pl.multiple_of(jb * tq, tq)
        if masked:
            row = lax.broadcasted_iota(jnp.int32, (tq, tq), 0)
            col = lax.broadcasted_iota(jnp.int32, (tq, tq), 1)
            keep = col <= row
        for h in range(2):
            lo = h * hd
            s = lax.dot_general(q_ref[:, lo:lo + hd], k_ref[pl.ds(r, tq), lo:lo + hd], nt,
                                preferred_element_type=F32)
            if masked:
                s = jnp.where(keep, s, MASK_VALUE)
            m_old = m_ref[h]
            m_new = jnp.maximum(m_old, jnp.max(s, axis=-1, keepdims=True))
            alpha = jnp.exp2(m_old - m_new)
            p = jnp.exp2(s - jnp.tile(m_new, (1, tq // LANES)))
            l_ref[h] = alpha * l_ref[h] + jnp.sum(p, axis=-1, keepdims=True)
            pv = jnp.dot(p.astype(BF16), v_ref[pl.ds(r, tq), :], preferred_element_type=F32)
            a_ref[h] = jnp.tile(alpha, (1, vd // LANES)) * a_ref[h] + pv
            m_ref[h] = m_new

    def loop_body(jb, carry):
        block(jb, False)
        return carry

    lax.fori_loop(0, i, loop_body, 0)
    block(i, True)

    lp = lam_ref[...]
    lam = (jnp.exp(jnp.sum(lp[0:1] * lp[1:2], axis=-1, keepdims=True))
           - jnp.exp(jnp.sum(lp[2:3] * lp[3:4], axis=-1, keepdims=True)) + np.float32(lam_init))
    o1 = a_ref[0] / jnp.tile(l_ref[0], (1, vd // LANES))
    o2 = a_ref[1] / jnp.tile(l_ref[1], (1, vd // LANES))
    o = o1 - lam * o2
    ms = jnp.mean(o * o, axis=-1, keepdims=True)
    o = (o * lax.rsqrt(ms + EPS)) * sg_ref[...] * np.float32(1.0 - lam_init)
    o_ref[...] = o.astype(o_ref.dtype)


def _diff_attention(q, kv, lam_params, subln_g, lam_init, n_heads, tq=512):
    b, s, d = q.shape
    vd = 2 * HEAD_DIM
    return pl.pallas_call(
        functools.partial(_attn_kernel, lam_init=lam_init, tq=tq),
        grid=(b, n_heads, s // tq),
        in_specs=[
            pl.BlockSpec((None, tq, vd), lambda bi, h, i: (bi, i, h)),
            pl.BlockSpec((None, s, vd), lambda bi, h, i: (bi, 0, h)),
            pl.BlockSpec((None, s, vd), lambda bi, h, i: (bi, 0, n_heads + h)),
            pl.BlockSpec((4, HEAD_DIM), lambda bi, h, i: (0, 0)),
            pl.BlockSpec((1, vd), lambda bi, h, i: (0, 0)),
        ],
        out_specs=pl.BlockSpec((None, tq, vd), lambda bi, h, i: (bi, i, h)),
        out_shape=jax.ShapeDtypeStruct((b, s, d), BF16),
        scratch_shapes=[pltpu.VMEM((2, tq, LANES), F32), pltpu.VMEM((2, tq, LANES), F32),
                        pltpu.VMEM((2, tq, vd), F32)],
        compiler_params=_cparams(3),
        name="diff_attention",
    )(q, kv, kv, lam_params, subln_g.reshape(1, vd))


def _oproj_kernel(a_ref, w_ref, x_ref, gate_ref, o_ref):
    y = jnp.dot(a_ref[...], w_ref[...].astype(BF16), preferred_element_type=F32)
    o_ref[...] = x_ref[...] + gate_ref[...] * y


def _oproj(a, w, x, gate, seq, tm=1024, tn=512):
    t, d = x.shape
    k = a.shape[1]
    tpb = seq // tm
    return pl.pallas_call(
        _oproj_kernel,
        grid=(t // tm, d // tn),
        in_specs=[
            pl.BlockSpec((tm, k), lambda i, j: (i, 0)),
            pl.BlockSpec((k, tn), lambda i, j: (0, j)),
            pl.BlockSpec((tm, tn), lambda i, j: (i, j)),
            pl.BlockSpec((None, 1, tn), lambda i, j: (i // tpb, 0, j)),
        ],
        out_specs=pl.BlockSpec((tm, tn), lambda i, j: (i, j)),
        out_shape=jax.ShapeDtypeStruct((t, d), F32),
        compiler_params=_cparams(2),
        name="out_proj",
    )(a, w, x, gate)


def kernel(x, c, positions, a_ada_w, a_ada_b, a_norm1_g, a_in_w, a_in_b, a_sgu_ln_g, a_sgu_ln_b, a_sgu_w, a_sgu_b, a_out_w, a_norm2_g, a_ffn_wi, a_ffn_wo, kv_ada_w, kv_ada_b, kv_norm_g, kv_w, k_norm_g, b_ada_w, b_ada_b, b_norm1_g, b_q_w, b_q_norm_g, b_lambda_q1, b_lambda_k1, b_lambda_q2, b_lambda_k2, b_subln_g, b_o_w, b_norm2_g, b_ffn_wi, b_ffn_wo):
    bsz, seq, d = x.shape
    n_a = a_ada_w.shape[0]
    n_b = b_ada_w.shape[0]
    n_heads = d // (2 * HEAD_DIM)
    t = bsz * seq

    cos2, sin2 = _rope_tables(positions)
    xf = x.reshape(t, d)

    for l in range(n_a):
        sh1, sc1, g1, sh2, sc2, g2 = _ada_mod(c, a_ada_w[l], a_ada_b[l], 6)
        xf = _gmlp_mix(xf, sh1, sc1, g1, a_norm1_g[l], a_in_w[l], a_in_b[l], a_sgu_ln_g[l],
                       a_sgu_ln_b[l], a_sgu_w[l], a_sgu_b[l], a_out_w[l], seq)
        xf = _ffn(xf, sh2, sc2, g2, a_norm2_g[l], a_ffn_wi[l], a_ffn_wo[l], seq)

    if n_b > 0:
        sh, sc = _ada_mod(c, kv_ada_w, kv_ada_b, 2)
        kv = _proj(xf, sh, sc, kv_norm_g, kv_w, k_norm_g, cos2, sin2, seq, rope_cols=d)
        kv = kv.reshape(bsz, seq, 2 * d)

    for jl in range(n_b):
        layer_idx = n_a + jl
        lam_init = 0.8 - 0.6 * math.exp(-0.3 * layer_idx)
        sh1, sc1, g1, sh2, sc2, g2 = _ada_mod(c, b_ada_w[jl], b_ada_b[jl], 6)
        q = _proj(xf, sh1, sc1, b_norm1_g[jl], b_q_w[jl], b_q_norm_g[jl], cos2, sin2, seq,
                  rope_cols=d, out_scale=HEAD_DIM ** -0.5 * math.log2(math.e))
        lam_params = jnp.stack([b_lambda_q1[jl], b_lambda_k1[jl],
                                b_lambda_q2[jl], b_lambda_k2[jl]])
        o = _diff_attention(q.reshape(bsz, seq, d), kv, lam_params, b_subln_g[jl], lam_init,
                            n_heads)
        xf = _oproj(o.reshape(t, d), b_o_w[jl], xf, g1, seq)
        xf = _ffn(xf, sh2, sc2, g2, b_norm2_g[jl], b_ffn_wi[jl], b_ffn_wo[jl], seq)

    return xf.reshape(bsz, seq, d)
```

```python
import functools
import math

import jax
import jax.numpy as jnp
import numpy as np
from jax import lax
from jax.experimental import pallas as pl
from jax.experimental.pallas import tpu as pltpu

F32 = jnp.float32
BF16 = jnp.bfloat16

EPS = 1e-6
ROPE_THETA = 10000.0
HEAD_DIM = 128
CHUNK = 128
LANES = 128
SUBLANES = 8
MXU_COLS = 256
VMEM_LIMIT_BYTES = 58 * 1024 * 1024
MASK_VALUE = -1e30
NORM_ROWS = 128


def _cparams(n_axes):
    return pltpu.CompilerParams(
        dimension_semantics=("arbitrary",) * n_axes,
        vmem_limit_bytes=VMEM_LIMIT_BYTES,
    )


def _sigmoid(x):
    return 1.0 / (1.0 + jnp.exp(-x))


def _norm_mod_store(x_ref, variants):
    tm = x_ref.shape[0]
    for g_ref, sh_ref, sc_ref, h_ref in variants:
        gain = g_ref[...] * (1.0 + sc_ref[...])
        shift = sh_ref[...]

        def body(c, carry, gain=gain, shift=shift, h_ref=h_ref):
            r = pl.multiple_of(c * NORM_ROWS, NORM_ROWS)
            x = x_ref[pl.ds(r, NORM_ROWS), :]
            y = x * lax.rsqrt(jnp.mean(x * x, axis=-1, keepdims=True) + EPS)
            h_ref[pl.ds(r, NORM_ROWS), :] = (y * gain + shift).astype(BF16)
            return carry

        lax.fori_loop(0, tm // NORM_ROWS, body, 0)


def _rope_kernel(pos_ref, freq_ref, cos_ref, sin_ref):
    ang = pos_ref[...].astype(F32) * freq_ref[...]
    cos_ref[...] = jnp.cos(ang)
    sin_ref[...] = jnp.sin(ang)


def _rope_tables(positions, tm=1024):
    t = positions.size
    inv_freq = 1.0 / (ROPE_THETA ** (jnp.arange(0, HEAD_DIM, 2, dtype=F32) / HEAD_DIM))
    tab = jnp.concatenate([inv_freq, inv_freq]).reshape(1, HEAD_DIM)
    pos = positions.reshape(t, 1)
    return pl.pallas_call(
        _rope_kernel,
        grid=(t // tm,),
        in_specs=[pl.BlockSpec((tm, 1), lambda i: (i, 0)),
                  pl.BlockSpec((1, HEAD_DIM), lambda i: (0, 0))],
        out_specs=[pl.BlockSpec((tm, HEAD_DIM), lambda i: (i, 0)),
                   pl.BlockSpec((tm, HEAD_DIM), lambda i: (i, 0))],
        out_shape=[jax.ShapeDtypeStruct((t, HEAD_DIM), F32)] * 2,
        compiler_params=_cparams(1),
        name="rope_tables",
    )(pos, tab)


def _ada_kernel(c_ref, w_ref, b_ref, o_ref, *, row_chunk):
    nb, k, _ = c_ref.shape
    tn = w_ref.shape[1]
    n_chunks = k // row_chunk

    def body(kk, accs):
        r = pl.multiple_of(kk * row_chunk, row_chunk)
        w = w_ref[pl.ds(r, row_chunk), :]
        new = []
        for b in range(nb):
            cb = c_ref[b, pl.ds(r, row_chunk), :]
            sb = cb * _sigmoid(cb)
            prod = (w * sb).reshape(row_chunk // SUBLANES, SUBLANES, tn)
            new.append(accs[b] + jnp.sum(prod, axis=0))
        return tuple(new)

    accs = lax.fori_loop(0, n_chunks, body,
                         tuple(jnp.zeros((SUBLANES, tn), F32) for _ in range(nb)))
    for b in range(nb):
        o_ref[b:b + 1, :] = jnp.sum(accs[b], axis=0, keepdims=True) + b_ref[...]


def _ada_mod(c, w, b, n, tn=1024, row_chunk=64):
    nb, k = c.shape
    nd = w.shape[1]
    out = pl.pallas_call(
        functools.partial(_ada_kernel, row_chunk=row_chunk),
        grid=(nd // tn,),
        in_specs=[pl.BlockSpec((nb, k, 1), lambda j: (0, 0, 0)),
                  pl.BlockSpec((k, tn), lambda j: (0, j)),
                  pl.BlockSpec((1, tn), lambda j: (0, j))],
        out_specs=pl.BlockSpec((nb, tn), lambda j: (0, j)),
        out_shape=jax.ShapeDtypeStruct((nb, nd), F32),
        compiler_params=_cparams(1),
        name="ada_mod",
    )(c.reshape(nb, k, 1), w, b.reshape(1, nd))
    d = nd // n
    return [out[:, i * d:(i + 1) * d].reshape(nb, 1, d) for i in range(n)]


def _gelu(z):
    return 0.5 * z * (1.0 + lax.erf(z * np.float32(math.sqrt(0.5))))


def _gmlp_kernel(x_ref, sh_ref, sc_ref, gate_ref, ng_ref, inw_ref, inb_ref, lng_ref, lnb_ref,
                 sguw_ref, sgub_ref, outw_ref, o_ref, h_ref, sv_ref, p_ref, wm_ref,
                 *, nv, tn1, tn2):
    j = pl.program_id(1)
    tm, d = x_ref.shape
    n_groups = sguw_ref.shape[0]

    @pl.when(j == 0)
    def _():
        _norm_mod_store(x_ref, [(ng_ref, sh_ref, sc_ref, h_ref)])

    def in_proj():
        z = jnp.dot(h_ref[...], inw_ref[...].astype(BF16), preferred_element_type=F32)
        return _gelu(z + inb_ref[...])

    @pl.when(j < nv)
    def _():
        sv_ref[:, pl.ds(pl.multiple_of(j * tn1, tn1), tn1)] = in_proj()

    @pl.when(j == nv)
    def _():
        row = lax.broadcasted_iota(jnp.int32, (CHUNK, CHUNK), 0)
        col = lax.broadcasted_iota(jnp.int32, (CHUNK, CHUNK), 1)
        for g in range(n_groups):
            wm_ref[g] = jnp.where(col <= row, sguw_ref[g], 0.0).astype(BF16)

        def chunk_body(cidx, carry):
            r = pl.multiple_of(cidx * CHUNK, CHUNK)
            v = sv_ref[pl.ds(r, CHUNK), :]
            mu = jnp.mean(v, axis=-1, keepdims=True)
            vc = v - mu
            var = jnp.mean(vc * vc, axis=-1, keepdims=True)
            vn = ((vc * lax.rsqrt(var + EPS)) * lng_ref[...] + lnb_ref[...]).astype(BF16)
            for g in range(n_groups):
                lo = g * CHUNK
                sv = jnp.dot(wm_ref[g], vn[:, lo:lo + CHUNK], preferred_element_type=F32)
                sv_ref[pl.ds(r, CHUNK), lo:lo + CHUNK] = sv + sgub_ref[:, lo:lo + CHUNK]
            return carry

        lax.fori_loop(0, tm // CHUNK, chunk_body, 0)

    @pl.when((j >= nv) & (j < 2 * nv))
    def _():
        c0 = pl.multiple_of((j - nv) * tn1, tn1)
        p_ref[:, pl.ds(c0, tn1)] = (in_proj() * sv_ref[:, pl.ds(c0, tn1)]).astype(BF16)

    @pl.when(j >= 2 * nv)
    def _():
        y = jnp.dot(p_ref[...], outw_ref[...].astype(BF16), preferred_element_type=F32)
        c0 = pl.multiple_of((j - 2 * nv) * tn2, tn2)
        o_ref[...] = x_ref[:, pl.ds(c0, tn2)] + gate_ref[:, pl.ds(c0, tn2)] * y


def _gmlp_mix(x, sh, sc, gate, norm_g, in_w, in_b, ln_g, ln_b, sgu_w, sgu_b, out_w, seq,
              tm=1024, tn1=512, tn2=512):
    t, d = x.shape
    nv = d // tn1
    n2 = d // tn2
    tpb = seq // tm
    n_groups = sgu_w.shape[0]
    sgub_full = jnp.repeat(sgu_b.T, d // n_groups, axis=1)
    vec = lambda: pl.BlockSpec((None, 1, d), lambda i, j: (i // tpb, 0, 0))
    row = lambda n: pl.BlockSpec((1, n), lambda i, j: (0, 0))

    def in_col(i, j):
        jj = jnp.minimum(j, 2 * nv - 1)
        return (0, jnp.where(jj < nv, jj + nv, jj - nv))

    out_col = lambda i, j: jnp.maximum(j - 2 * nv, 0)
    return pl.pallas_call(
        functools.partial(_gmlp_kernel, nv=nv, tn1=tn1, tn2=tn2),
        grid=(t // tm, 2 * nv + n2),
        in_specs=[
            pl.BlockSpec((tm, d), lambda i, j: (i, 0), pipeline_mode=pl.Buffered(1)),
            vec(), vec(), vec(), row(d),
            pl.BlockSpec((d, tn1), in_col),
            pl.BlockSpec((1, tn1), in_col),
            row(d), row(d),
            pl.BlockSpec((n_groups, CHUNK, CHUNK), lambda i, j: (0, 0, 0)),
            pl.BlockSpec((CHUNK, d), lambda i, j: (0, 0)),
            pl.BlockSpec((d, tn2), lambda i, j: (0, out_col(i, j))),
        ],
        out_specs=pl.BlockSpec((tm, tn2), lambda i, j: (i, out_col(i, j))),
        out_shape=jax.ShapeDtypeStruct((t, d), F32),
        scratch_shapes=[pltpu.VMEM((tm, d), BF16),
                        pltpu.VMEM((tm, d), F32),
                        pltpu.VMEM((tm, d), BF16),
                        pltpu.VMEM((n_groups, CHUNK, CHUNK), BF16)],
        compiler_params=_cparams(2),
        name="gmlp_mix",
    )(x, sh, sc, gate, norm_g.reshape(1, d), in_w, in_b.reshape(1, 2 * d),
      ln_g.reshape(1, d), ln_b.reshape(1, d), sgu_w, sgub_full, out_w)


def _ffn_kernel(x_ref, sh_ref, sc_ref, gate_ref, ng_ref, wg_ref, wu_ref, wo_ref, o_ref, h_ref):
    j = pl.program_id(1)

    @pl.when(j == 0)
    def _():
        _norm_mod_store(x_ref, [(ng_ref, sh_ref, sc_ref, h_ref)])
        o_ref[...] = jnp.zeros(o_ref.shape, F32)

    h = h_ref[...]
    g = jnp.dot(h, wg_ref[...].astype(BF16), preferred_element_type=F32)
    u = jnp.dot(h, wu_ref[...].astype(BF16), preferred_element_type=F32)
    a = (g * _sigmoid(g) * u).astype(BF16)
    o_ref[...] += jnp.dot(a, wo_ref[...].astype(BF16), preferred_element_type=F32)

    @pl.when(j == pl.num_programs(1) - 1)
    def _():
        o_ref[...] = x_ref[...] + gate_ref[...] * o_ref[...]


def _ffn(x, sh, sc, gate, norm_g, wi, wo, seq, tm=1024, tf=256):
    t, d = x.shape
    f = wo.shape[0]
    nf = f // tf
    tpb = seq // tm
    vec = lambda: pl.BlockSpec((None, 1, d), lambda i, j: (i // tpb, 0, 0))
    return pl.pallas_call(
        _ffn_kernel,
        grid=(t // tm, nf),
        in_specs=[
            pl.BlockSpec((tm, d), lambda i, j: (i, 0), pipeline_mode=pl.Buffered(1)),
            vec(), vec(), vec(),
            pl.BlockSpec((1, d), lambda i, j: (0, 0)),
            pl.BlockSpec((d, tf), lambda i, j: (0, j)),
            pl.BlockSpec((d, tf), lambda i, j: (0, j + nf)),
            pl.BlockSpec((tf, d), lambda i, j: (j, 0)),
        ],
        out_specs=pl.BlockSpec((tm, d), lambda i, j: (i, 0)),
        out_shape=jax.ShapeDtypeStruct((t, d), F32),
        scratch_shapes=[pltpu.VMEM((tm, d), BF16)],
        compiler_params=_cparams(2),
        name="swiglu_ffn",
    )(x, sh, sc, gate, norm_g.reshape(1, d), wi, wi, wo)


def _pair_split_cols(w):
    d, n = w.shape
    half = HEAD_DIM // 2
    w = w.reshape(d, n // (2 * HEAD_DIM), 2, 2, half).transpose(0, 1, 3, 2, 4)
    return w.reshape(d, n).astype(BF16)


def _pair_split_gain(g, scale):
    half = HEAD_DIM // 2
    g = g * np.float32(scale)
    return jnp.stack([jnp.tile(g[:half], 2), jnp.tile(g[half:], 2)])


def _qkv_kernel(x_ref, shk_ref, sck_ref, ngk_ref, shq_ref, scq_ref, ngq_ref, wk_ref, wv_ref,
                wq_ref, hgk_ref, hgq_ref, cos_ref, sin_ref, o_ref, hkv_ref, hq_ref,
                *, nk, nv):
    j = pl.program_id(1)
    tn = o_ref.shape[1]

    @pl.when(j == 0)
    def _():
        _norm_mod_store(x_ref, [(ngk_ref, shk_ref, sck_ref, hkv_ref),
                                (ngq_ref, shq_ref, scq_ref, hq_ref)])

    def project_rope(h_ref, w_ref, hg_ref):
        cos = cos_ref[...]
        sin = sin_ref[...]
        g_lo = hg_ref[0:1, :]
        g_hi = hg_ref[1:2, :]
        t_ll, t_lh = cos * g_lo, -(sin * g_hi)
        t_hh, t_hl = cos * g_hi, sin * g_lo
        is_a = lax.broadcasted_iota(jnp.int32, (1, HEAD_DIM), 1) < HEAD_DIM // 2
        inv_n = np.float32(1.0 / HEAD_DIM)
        for c in range(tn // MXU_COLS):
            c0 = c * MXU_COLS
            y = jnp.dot(h_ref[...], w_ref[:, c0:c0 + MXU_COLS], preferred_element_type=F32)
            y_lo = y[:, :HEAD_DIM]
            y_hi = y[:, HEAD_DIM:]
            sq = y_lo * y_lo + y_hi * y_hi
            sum_a = jnp.sum(jnp.where(is_a, sq, 0.0), axis=-1, keepdims=True)
            sum_b = jnp.sum(sq, axis=-1, keepdims=True) - sum_a
            rs = jnp.where(is_a, lax.rsqrt(sum_a * inv_n + EPS), lax.rsqrt(sum_b * inv_n + EPS))
            o_ref[:, c0:c0 + HEAD_DIM] = ((y_lo * t_ll + y_hi * t_lh) * rs).astype(o_ref.dtype)
            o_ref[:, c0 + HEAD_DIM:c0 + MXU_COLS] = (
                (y_hi * t_hh + y_lo * t_hl) * rs).astype(o_ref.dtype)

    @pl.when(j < nk)
    def _():
        project_rope(hkv_ref, wk_ref, hgk_ref)

    @pl.when((j >= nk) & (j < nk + nv))
    def _():
        for c in range(tn // MXU_COLS):
            c0 = c * MXU_COLS
            y = jnp.dot(hkv_ref[...], wv_ref[:, c0:c0 + MXU_COLS].astype(BF16),
                        preferred_element_type=F32)
            o_ref[:, c0:c0 + MXU_COLS] = y.astype(o_ref.dtype)

    @pl.when(j >= nk + nv)
    def _():
        project_rope(hq_ref, wq_ref, hgq_ref)


def _qkv_proj(x, shk, sck, ngk, shq, scq, ngq, kv_w, q_w, k_norm_g, q_norm_g, cos2, sin2, seq,
              q_scale, tm=1024, tn=512):
    t, d = x.shape
    nk = d // tn
    nv = (kv_w.shape[1] - d) // tn
    nq = q_w.shape[1] // tn
    tpb = seq // tm
    wk = _pair_split_cols(kv_w[:, :d])
    wq = _pair_split_cols(q_w)
    vec = lambda: pl.BlockSpec((None, 1, d), lambda i, j: (i // tpb, 0, 0))
    row = lambda n: pl.BlockSpec((1, n), lambda i, j: (0, 0))
    gain = lambda: pl.BlockSpec((2, HEAD_DIM), lambda i, j: (0, 0))
    tab = lambda: pl.BlockSpec((tm, HEAD_DIM), lambda i, j: (i, 0))
    return pl.pallas_call(
        functools.partial(_qkv_kernel, nk=nk, nv=nv),
        grid=(t // tm, nk + nv + nq),
        in_specs=[
            pl.BlockSpec((tm, d), lambda i, j: (i, 0)),
            vec(), vec(), row(d), vec(), vec(), row(d),
            pl.BlockSpec((d, tn), lambda i, j: (0, jnp.minimum(j, nk - 1))),
            pl.BlockSpec((d, tn), lambda i, j: (0, jnp.clip(j, nk, nk + nv - 1))),
            pl.BlockSpec((d, tn), lambda i, j: (0, jnp.maximum(j - nk - nv, 0))),
            gain(), gain(), tab(), tab(),
        ],
        out_specs=pl.BlockSpec((tm, tn), lambda i, j: (i, j)),
        out_shape=jax.ShapeDtypeStruct((t, kv_w.shape[1] + q_w.shape[1]), BF16),
        scratch_shapes=[pltpu.VMEM((tm, d), BF16), pltpu.VMEM((tm, d), BF16)],
        compiler_params=_cparams(2),
        name="qkv_proj",
    )(x, shk, sck, ngk.reshape(1, d), shq, scq, ngq.reshape(1, d), wk, kv_w, wq,
      _pair_split_gain(k_norm_g, 1.0), _pair_split_gain(q_norm_g, q_scale), cos2, sin2)


def _attn_kernel(q_ref, k_ref, v_ref, lam_ref, sg_ref, o_ref, qs_ref, m_ref, l_ref, a_ref,
                 *, lam_init, tq):
    i = pl.program_id(2)
    vd = a_ref.shape[-1]
    nt = (((1,), (1,)), ((), ()))

    lane = lax.broadcasted_iota(jnp.int32, (1, vd), 1)
    is_a = (lane % HEAD_DIM) < HEAD_DIM // 2
    q = q_ref[...]
    zero = jnp.zeros_like(q)
    qs_ref[0] = jnp.where(is_a, q, zero)
    qs_ref[1] = jnp.where(is_a, zero, q)
    m_ref[...] = jnp.full(m_ref.shape, MASK_VALUE, F32)
    l_ref[...] = jnp.zeros(l_ref.shape, F32)
    a_ref[...] = jnp.zeros(a_ref.shape, F32)

    def block(jb, masked):
        r = pl.multiple_of(jb * tq, tq)
        if masked:
            row = lax.broadcasted_iota(jnp.int32, (tq, tq), 0)
            col = lax.broadcasted_iota(jnp.int32, (tq, tq), 1)
            keep = col <= row
        for h in range(2):
            s = lax.dot_general(qs_ref[h], k_ref[pl.ds(r, tq), :], nt,
                                preferred_element_type=F32)
            if masked:
                s = jnp.where(keep, s, MASK_VALUE)
            m_old = m_ref[h]
            m_new = jnp.maximum(m_old, jnp.max(s, axis=-1, keepdims=True))
            alpha = jnp.exp2(m_old - m_new)
            p = jnp.exp2(s - jnp.tile(m_new, (1, tq // LANES)))
            l_ref[h] = alpha * l_ref[h] + jnp.sum(p, axis=-1, keepdims=True)
            pv = jnp.dot(p.astype(BF16), v_ref[pl.ds(r, tq), :], preferred_element_type=F32)
            a_ref[h] = jnp.tile(alpha, (1, vd // LANES)) * a_ref[h] + pv
            m_ref[h] = m_new

    def loop_body(jb, carry):
        block(jb, False)
        return carry

    lax.fori_loop(0, i, loop_body, 0)
    block(i, True)

    lp = lam_ref[...]
    lam = (jnp.exp(jnp.sum(lp[0:1] * lp[1:2], axis=-1, keepdims=True))
           - jnp.exp(jnp.sum(lp[2:3] * lp[3:4], axis=-1, keepdims=True)) + np.float32(lam_init))
    o1 = a_ref[0] / jnp.tile(l_ref[0], (1, vd // LANES))
    o2 = a_ref[1] / jnp.tile(l_ref[1], (1, vd // LANES))
    o = o1 - lam * o2
    ms = jnp.mean(o * o, axis=-1, keepdims=True)
    o = (o * lax.rsqrt(ms + EPS)) * sg_ref[...] * np.float32(1.0 - lam_init)
    o_ref[...] = o.astype(o_ref.dtype)


def _diff_attention(qkv, lam_params, subln_g, lam_init, n_heads, tq=512):
    b, s, d3 = qkv.shape
    d = d3 // 3
    vd = 2 * HEAD_DIM
    return pl.pallas_call(
        functools.partial(_attn_kernel, lam_init=lam_init, tq=tq),
        grid=(b, n_heads, s // tq),
        in_specs=[
            pl.BlockSpec((None, tq, vd), lambda bi, h, i: (bi, i, 2 * n_heads + h)),
            pl.BlockSpec((None, s, vd), lambda bi, h, i: (bi, 0, h)),
            pl.BlockSpec((None, s, vd), lambda bi, h, i: (bi, 0, n_heads + h)),
            pl.BlockSpec((4, HEAD_DIM), lambda bi, h, i: (0, 0)),
            pl.BlockSpec((1, vd), lambda bi, h, i: (0, 0)),
        ],
        out_specs=pl.BlockSpec((None, tq, vd), lambda bi, h, i: (bi, i, h)),
        out_shape=jax.ShapeDtypeStruct((b, s, d), BF16),
        scratch_shapes=[pltpu.VMEM((2, tq, vd), BF16),
                        pltpu.VMEM((2, tq, LANES), F32), pltpu.VMEM((2, tq, LANES), F32),
                        pltpu.VMEM((2, tq, vd), F32)],
        compiler_params=_cparams(3),
        name="diff_attention",
    )(qkv, qkv, qkv, lam_params, subln_g.reshape(1, vd))


def _oproj_kernel(a_ref, w_ref, x_ref, gate_ref, o_ref):
    y = jnp.dot(a_ref[...], w_ref[...].astype(BF16), preferred_element_type=F32)
    o_ref[...] = x_ref[...] + gate_ref[...] * y


def _oproj(a, w, x, gate, seq, tm=1024, tn=1024):
    t, d = x.shape
    k = a.shape[1]
    tpb = seq // tm
    return pl.pallas_call(
        _oproj_kernel,
        grid=(d // tn, t // tm),
        in_specs=[
            pl.BlockSpec((tm, k), lambda j, i: (i, 0)),
            pl.BlockSpec((k, tn), lambda j, i: (0, j)),
            pl.BlockSpec((tm, tn), lambda j, i: (i, j)),
            pl.BlockSpec((None, 1, tn), lambda j, i: (i // tpb, 0, j)),
        ],
        out_specs=pl.BlockSpec((tm, tn), lambda j, i: (i, j)),
        out_shape=jax.ShapeDtypeStruct((t, d), F32),
        compiler_params=_cparams(2),
        name="out_proj",
    )(a, w, x, gate)


def kernel(x, c, positions, a_ada_w, a_ada_b, a_norm1_g, a_in_w, a_in_b, a_sgu_ln_g, a_sgu_ln_b, a_sgu_w, a_sgu_b, a_out_w, a_norm2_g, a_ffn_wi, a_ffn_wo, kv_ada_w, kv_ada_b, kv_norm_g, kv_w, k_norm_g, b_ada_w, b_ada_b, b_norm1_g, b_q_w, b_q_norm_g, b_lambda_q1, b_lambda_k1, b_lambda_q2, b_lambda_k2, b_subln_g, b_o_w, b_norm2_g, b_ffn_wi, b_ffn_wo):
    bsz, seq, d = x.shape
    n_a = a_ada_w.shape[0]
    n_b = b_ada_w.shape[0]
    n_heads = d // (2 * HEAD_DIM)
    t = bsz * seq
    q_scale = HEAD_DIM ** -0.5 * math.log2(math.e)

    cos2, sin2 = _rope_tables(positions)
    xf = x.reshape(t, d)

    for l in range(n_a):
        sh1, sc1, g1, sh2, sc2, g2 = _ada_mod(c, a_ada_w[l], a_ada_b[l], 6)
        xf = _gmlp_mix(xf, sh1, sc1, g1, a_norm1_g[l], a_in_w[l], a_in_b[l], a_sgu_ln_g[l],
                       a_sgu_ln_b[l], a_sgu_w[l], a_sgu_b[l], a_out_w[l], seq)
        xf = _ffn(xf, sh2, sc2, g2, a_norm2_g[l], a_ffn_wi[l], a_ffn_wo[l], seq)

    assert n_b == 1, "only one differential-attention layer is supported"
    shk, sck = _ada_mod(c, kv_ada_w, kv_ada_b, 2)

    for jl in range(n_b):
        layer_idx = n_a + jl
        lam_init = 0.8 - 0.6 * math.exp(-0.3 * layer_idx)
        sh1, sc1, g1, sh2, sc2, g2 = _ada_mod(c, b_ada_w[jl], b_ada_b[jl], 6)
        qkv = _qkv_proj(xf, shk, sck, kv_norm_g, sh1, sc1, b_norm1_g[jl], kv_w, b_q_w[jl],
                        k_norm_g, b_q_norm_g[jl], cos2, sin2, seq, q_scale)
        lam_params = jnp.stack([b_lambda_q1[jl], b_lambda_k1[jl],
                                b_lambda_q2[jl], b_lambda_k2[jl]])
        o = _diff_attention(qkv.reshape(bsz, seq, 3 * d), lam_params, b_subln_g[jl], lam_init,
                            n_heads)
        xf = _oproj(o.reshape(t, d), b_o_w[jl], xf, g1, seq)
        xf = _ffn(xf, sh2, sc2, g2, b_norm2_g[jl], b_ffn_wi[jl], b_ffn_wo[jl], seq)

    return xf.reshape(bsz, seq, d)
```

```python
import functools
import math

import jax
import jax.numpy as jnp
import numpy as np
from jax import lax
from jax.experimental import pallas as pl
from jax.experimental.pallas import tpu as pltpu

F32 = jnp.float32
BF16 = jnp.bfloat16

EPS = 1e-6
ROPE_THETA = 10000.0
HEAD_DIM = 128
CHUNK = 128
LANES = 128
SUBLANES = 8
MXU_COLS = 256
VMEM_LIMIT_BYTES = 58 * 1024 * 1024
MASK_VALUE = -1e30
NORM_ROWS = 128


def _cparams(n_axes):
    return pltpu.CompilerParams(
        dimension_semantics=("arbitrary",) * n_axes,
        vmem_limit_bytes=VMEM_LIMIT_BYTES,
    )


def _sigmoid(x):
    return 1.0 / (1.0 + jnp.exp(-x))


def _norm_mod_store(x_ref, variants):
    tm = x_ref.shape[0]
    for g_ref, sh_ref, sc_ref, h_ref in variants:
        gain = g_ref[...] * (1.0 + sc_ref[...])
        shift = sh_ref[...]

        def body(c, carry, gain=gain, shift=shift, h_ref=h_ref):
            r = pl.multiple_of(c * NORM_ROWS, NORM_ROWS)
            x = x_ref[pl.ds(r, NORM_ROWS), :]
            y = x * lax.rsqrt(jnp.mean(x * x, axis=-1, keepdims=True) + EPS)
            h_ref[pl.ds(r, NORM_ROWS), :] = (y * gain + shift).astype(BF16)
            return carry

        lax.fori_loop(0, tm // NORM_ROWS, body, 0)


def _rope_kernel(pos_ref, freq_ref, cos_ref, sin_ref):
    ang = pos_ref[...].astype(F32) * freq_ref[...]
    cos_ref[...] = jnp.cos(ang)
    sin_ref[...] = jnp.sin(ang)


def _rope_tables(positions, tm=1024):
    t = positions.size
    inv_freq = 1.0 / (ROPE_THETA ** (jnp.arange(0, HEAD_DIM, 2, dtype=F32) / HEAD_DIM))
    tab = jnp.concatenate([inv_freq, inv_freq]).reshape(1, HEAD_DIM)
    pos = positions.reshape(t, 1)
    return pl.pallas_call(
        _rope_kernel,
        grid=(t // tm,),
        in_specs=[pl.BlockSpec((tm, 1), lambda i: (i, 0)),
                  pl.BlockSpec((1, HEAD_DIM), lambda i: (0, 0))],
        out_specs=[pl.BlockSpec((tm, HEAD_DIM), lambda i: (i, 0)),
                   pl.BlockSpec((tm, HEAD_DIM), lambda i: (i, 0))],
        out_shape=[jax.ShapeDtypeStruct((t, HEAD_DIM), F32)] * 2,
        compiler_params=_cparams(1),
        name="rope_tables",
    )(pos, tab)


def _ada_kernel(c_ref, w_ref, b_ref, o_ref, s_ref, *, row_chunk):
    nb, k, _ = c_ref.shape
    tn = w_ref.shape[1]
    n_chunks = k // row_chunk

    @pl.when(pl.program_id(0) == 0)
    def _():
        for b in range(nb):
            cb = c_ref[b]
            s_ref[b] = jnp.broadcast_to(cb * _sigmoid(cb), (k, LANES))

    def body(kk, accs):
        r = pl.multiple_of(kk * row_chunk, row_chunk)
        w = w_ref[pl.ds(r, row_chunk), :]
        new = []
        for b in range(nb):
            sb = jnp.tile(s_ref[b, pl.ds(r, row_chunk), :], (1, tn // LANES))
            prod = (w * sb).reshape(row_chunk // SUBLANES, SUBLANES, tn)
            new.append(accs[b] + jnp.sum(prod, axis=0))
        return tuple(new)

    accs = lax.fori_loop(0, n_chunks, body,
                         tuple(jnp.zeros((SUBLANES, tn), F32) for _ in range(nb)))
    for b in range(nb):
        o_ref[b:b + 1, :] = jnp.sum(accs[b], axis=0, keepdims=True) + b_ref[...]


def _ada_mod(c, w, b, n, tn=1024, row_chunk=64):
    nb, k = c.shape
    nd = w.shape[1]
    out = pl.pallas_call(
        functools.partial(_ada_kernel, row_chunk=row_chunk),
        grid=(nd // tn,),
        in_specs=[pl.BlockSpec((nb, k, 1), lambda j: (0, 0, 0)),
                  pl.BlockSpec((k, tn), lambda j: (0, j)),
                  pl.BlockSpec((1, tn), lambda j: (0, j))],
        out_specs=pl.BlockSpec((nb, tn), lambda j: (0, j)),
        out_shape=jax.ShapeDtypeStruct((nb, nd), F32),
        scratch_shapes=[pltpu.VMEM((nb, k, LANES), F32)],
        compiler_params=_cparams(1),
        name="ada_mod",
    )(c.reshape(nb, k, 1), w, b.reshape(1, nd))
    d = nd // n
    return [out[:, i * d:(i + 1) * d].reshape(nb, 1, d) for i in range(n)]


def _gelu(z):
    return 0.5 * z * (1.0 + lax.erf(z * np.float32(math.sqrt(0.5))))


def _gmlp_kernel(x_ref, sh_ref, sc_ref, gate_ref, ng_ref, inw_ref, inb_ref, lng_ref, lnb_ref,
                 sguw_ref, sgub_ref, outw_ref, o_ref, h_ref, sv_ref, p_ref, wm_ref,
                 *, nv, tn1, tn2):
    j = pl.program_id(1)
    tm, d = x_ref.shape
    n_groups = sguw_ref.shape[0]

    @pl.when(j == 0)
    def _():
        _norm_mod_store(x_ref, [(ng_ref, sh_ref, sc_ref, h_ref)])

    def in_proj():
        z = jnp.dot(h_ref[...], inw_ref[...].astype(BF16), preferred_element_type=F32)
        return _gelu(z + inb_ref[...])

    @pl.when(j < nv)
    def _():
        sv_ref[:, pl.ds(pl.multiple_of(j * tn1, tn1), tn1)] = in_proj()

    @pl.when(j == nv)
    def _():
        row = lax.broadcasted_iota(jnp.int32, (CHUNK, CHUNK), 0)
        col = lax.broadcasted_iota(jnp.int32, (CHUNK, CHUNK), 1)
        for g in range(n_groups):
            wm_ref[g] = jnp.where(col <= row, sguw_ref[g], 0.0).astype(BF16)

        def chunk_body(cidx, carry):
            r = pl.multiple_of(cidx * CHUNK, CHUNK)
            v = sv_ref[pl.ds(r, CHUNK), :]
            mu = jnp.mean(v, axis=-1, keepdims=True)
            vc = v - mu
            var = jnp.mean(vc * vc, axis=-1, keepdims=True)
            vn = ((vc * lax.rsqrt(var + EPS)) * lng_ref[...] + lnb_ref[...]).astype(BF16)
            for g in range(n_groups):
                lo = g * CHUNK
                sv = jnp.dot(wm_ref[g], vn[:, lo:lo + CHUNK], preferred_element_type=F32)
                sv_ref[pl.ds(r, CHUNK), lo:lo + CHUNK] = sv + sgub_ref[:, lo:lo + CHUNK]
            return carry

        lax.fori_loop(0, tm // CHUNK, chunk_body, 0)

    @pl.when((j >= nv) & (j < 2 * nv))
    def _():
        c0 = pl.multiple_of((j - nv) * tn1, tn1)
        p_ref[:, pl.ds(c0, tn1)] = (in_proj() * sv_ref[:, pl.ds(c0, tn1)]).astype(BF16)

    @pl.when(j >= 2 * nv)
    def _():
        y = jnp.dot(p_ref[...], outw_ref[...].astype(BF16), preferred_element_type=F32)
        c0 = pl.multiple_of((j - 2 * nv) * tn2, tn2)
        o_ref[...] = x_ref[:, pl.ds(c0, tn2)] + gate_ref[:, pl.ds(c0, tn2)] * y


def _gmlp_mix(x, sh, sc, gate, norm_g, in_w, in_b, ln_g, ln_b, sgu_w, sgu_b, out_w, seq,
              tm=1024, tn1=1024, tn2=512):
    t, d = x.shape
    nv = d // tn1
    n2 = d // tn2
    tpb = seq // tm
    n_groups = sgu_w.shape[0]
    sgub_full = jnp.repeat(sgu_b.T, d // n_groups, axis=1)
    vec = lambda: pl.BlockSpec((None, 1, d), lambda i, j: (i // tpb, 0, 0))
    row = lambda n: pl.BlockSpec((1, n), lambda i, j: (0, 0))

    def in_col(i, j):
        jj = jnp.minimum(j, 2 * nv - 1)
        return (0, jnp.where(jj < nv, jj + nv, jj - nv))

    out_col = lambda i, j: jnp.maximum(j - 2 * nv, 0)
    return pl.pallas_call(
        functools.partial(_gmlp_kernel, nv=nv, tn1=tn1, tn2=tn2),
        grid=(t // tm, 2 * nv + n2),
        in_specs=[
            pl.BlockSpec((tm, d), lambda i, j: (i, 0), pipeline_mode=pl.Buffered(1)),
            vec(), vec(), vec(), row(d),
            pl.BlockSpec((d, tn1), in_col),
            pl.BlockSpec((1, tn1), in_col),
            row(d), row(d),
            pl.BlockSpec((n_groups, CHUNK, CHUNK), lambda i, j: (0, 0, 0)),
            pl.BlockSpec((CHUNK, d), lambda i, j: (0, 0)),
            pl.BlockSpec((d, tn2), lambda i, j: (0, out_col(i, j))),
        ],
        out_specs=pl.BlockSpec((tm, tn2), lambda i, j: (i, out_col(i, j))),
        out_shape=jax.ShapeDtypeStruct((t, d), F32),
        scratch_shapes=[pltpu.VMEM((tm, d), BF16),
                        pltpu.VMEM((tm, d), F32),
                        pltpu.VMEM((tm, d), BF16),
                        pltpu.VMEM((n_groups, CHUNK, CHUNK), BF16)],
        compiler_params=_cparams(2),
        name="gmlp_mix",
    )(x, sh, sc, gate, norm_g.reshape(1, d), in_w.astype(BF16), in_b.reshape(1, 2 * d),
      ln_g.reshape(1, d), ln_b.reshape(1, d), sgu_w, sgub_full, out_w.astype(BF16))


def _ffn_kernel(x_ref, sh_ref, sc_ref, gate_ref, ng_ref, wg_ref, wu_ref, wo_ref, o_ref, h_ref):
    j = pl.program_id(1)

    @pl.when(j == 0)
    def _():
        _norm_mod_store(x_ref, [(ng_ref, sh_ref, sc_ref, h_ref)])
        o_ref[...] = jnp.zeros(o_ref.shape, F32)

    h = h_ref[...]
    g = jnp.dot(h, wg_ref[...].astype(BF16), preferred_element_type=F32)
    u = jnp.dot(h, wu_ref[...].astype(BF16), preferred_element_type=F32)
    a = (g * _sigmoid(g) * u).astype(BF16)
    o_ref[...] += jnp.dot(a, wo_ref[...].astype(BF16), preferred_element_type=F32)

    @pl.when(j == pl.num_programs(1) - 1)
    def _():
        o_ref[...] = x_ref[...] + gate_ref[...] * o_ref[...]


def _ffn(x, sh, sc, gate, norm_g, wi, wo, seq, tm=1024, tf=256):
    t, d = x.shape
    f = wo.shape[0]
    nf = f // tf
    tpb = seq // tm
    vec = lambda: pl.BlockSpec((None, 1, d), lambda i, j: (i // tpb, 0, 0))
    return pl.pallas_call(
        _ffn_kernel,
        grid=(t // tm, nf),
        in_specs=[
            pl.BlockSpec((tm, d), lambda i, j: (i, 0), pipeline_mode=pl.Buffered(1)),
            vec(), vec(), vec(),
            pl.BlockSpec((1, d), lambda i, j: (0, 0)),
            pl.BlockSpec((d, tf), lambda i, j: (0, j)),
            pl.BlockSpec((d, tf), lambda i, j: (0, j + nf)),
            pl.BlockSpec((tf, d), lambda i, j: (j, 0)),
        ],
        out_specs=pl.BlockSpec((tm, d), lambda i, j: (i, 0)),
        out_shape=jax.ShapeDtypeStruct((t, d), F32),
        scratch_shapes=[pltpu.VMEM((tm, d), BF16)],
        compiler_params=_cparams(2),
        name="swiglu_ffn",
    )(x, sh, sc, gate, norm_g.reshape(1, d), wi, wi, wo)


def _pair_split_cols(w):
    d, n = w.shape
    half = HEAD_DIM // 2
    w = w.reshape(d, n // (2 * HEAD_DIM), 2, 2, half).transpose(0, 1, 3, 2, 4)
    return w.reshape(d, n).astype(BF16)


def _pair_split_gain(g, scale):
    half = HEAD_DIM // 2
    g = g * np.float32(scale)
    return jnp.stack([jnp.tile(g[:half], 2), jnp.tile(g[half:], 2)])


def _qkv_kernel(x_ref, shk_ref, sck_ref, ngk_ref, shq_ref, scq_ref, ngq_ref, wk_ref, wv_ref,
                wq_ref, hgk_ref, hgq_ref, cos_ref, sin_ref, o_ref, hkv_ref, hq_ref,
                *, nk, nv):
    j = pl.program_id(1)
    tn = o_ref.shape[1]

    @pl.when(j == 0)
    def _():
        _norm_mod_store(x_ref, [(ngk_ref, shk_ref, sck_ref, hkv_ref),
                                (ngq_ref, shq_ref, scq_ref, hq_ref)])

    def project_rope(h_ref, w_ref, hg_ref):
        cos = cos_ref[...]
        sin = sin_ref[...]
        g_lo = hg_ref[0:1, :]
        g_hi = hg_ref[1:2, :]
        t_ll, t_lh = cos * g_lo, -(sin * g_hi)
        t_hh, t_hl = cos * g_hi, sin * g_lo
        is_a = lax.broadcasted_iota(jnp.int32, (1, HEAD_DIM), 1) < HEAD_DIM // 2
        inv_n = np.float32(1.0 / HEAD_DIM)
        for c in range(tn // MXU_COLS):
            c0 = c * MXU_COLS
            y = jnp.dot(h_ref[...], w_ref[:, c0:c0 + MXU_COLS], preferred_element_type=F32)
            y_lo = y[:, :HEAD_DIM]
            y_hi = y[:, HEAD_DIM:]
            sq = y_lo * y_lo + y_hi * y_hi
            sum_a = jnp.sum(jnp.where(is_a, sq, 0.0), axis=-1, keepdims=True)
            sum_b = jnp.sum(sq, axis=-1, keepdims=True) - sum_a
            rs = jnp.where(is_a, lax.rsqrt(sum_a * inv_n + EPS), lax.rsqrt(sum_b * inv_n + EPS))
            o_ref[:, c0:c0 + HEAD_DIM] = ((y_lo * t_ll + y_hi * t_lh) * rs).astype(o_ref.dtype)
            o_ref[:, c0 + HEAD_DIM:c0 + MXU_COLS] = (
                (y_hi * t_hh + y_lo * t_hl) * rs).astype(o_ref.dtype)

    @pl.when(j < nk)
    def _():
        project_rope(hkv_ref, wk_ref, hgk_ref)

    @pl.when((j >= nk) & (j < nk + nv))
    def _():
        for c in range(tn // MXU_COLS):
            c0 = c * MXU_COLS
            y = jnp.dot(hkv_ref[...], wv_ref[:, c0:c0 + MXU_COLS].astype(BF16),
                        preferred_element_type=F32)
            o_ref[:, c0:c0 + MXU_COLS] = y.astype(o_ref.dtype)

    @pl.when(j >= nk + nv)
    def _():
        project_rope(hq_ref, wq_ref, hgq_ref)


def _qkv_proj(x, shk, sck, ngk, shq, scq, ngq, kv_w, q_w, k_norm_g, q_norm_g, cos2, sin2, seq,
              q_scale, tm=1024, tn=512):
    t, d = x.shape
    nk = d // tn
    nv = (kv_w.shape[1] - d) // tn
    nq = q_w.shape[1] // tn
    tpb = seq // tm
    wk = _pair_split_cols(kv_w[:, :d])
    wq = _pair_split_cols(q_w)
    vec = lambda: pl.BlockSpec((None, 1, d), lambda i, j: (i // tpb, 0, 0))
    row = lambda n: pl.BlockSpec((1, n), lambda i, j: (0, 0))
    gain = lambda: pl.BlockSpec((2, HEAD_DIM), lambda i, j: (0, 0))
    tab = lambda: pl.BlockSpec((tm, HEAD_DIM), lambda i, j: (i, 0))
    return pl.pallas_call(
        functools.partial(_qkv_kernel, nk=nk, nv=nv),
        grid=(t // tm, nk + nv + nq),
        in_specs=[
            pl.BlockSpec((tm, d), lambda i, j: (i, 0)),
            vec(), vec(), row(d), vec(), vec(), row(d),
            pl.BlockSpec((d, tn), lambda i, j: (0, jnp.minimum(j, nk - 1))),
            pl.BlockSpec((d, tn), lambda i, j: (0, jnp.clip(j, nk, nk + nv - 1))),
            pl.BlockSpec((d, tn), lambda i, j: (0, jnp.maximum(j - nk - nv, 0))),
            gain(), gain(), tab(), tab(),
        ],
        out_specs=pl.BlockSpec((tm, tn), lambda i, j: (i, j)),
        out_shape=jax.ShapeDtypeStruct((t, kv_w.shape[1] + q_w.shape[1]), BF16),
        scratch_shapes=[pltpu.VMEM((tm, d), BF16), pltpu.VMEM((tm, d), BF16)],
        compiler_params=_cparams(2),
        name="qkv_proj",
    )(x, shk, sck, ngk.reshape(1, d), shq, scq, ngq.reshape(1, d), wk, kv_w, wq,
      _pair_split_gain(k_norm_g, 1.0), _pair_split_gain(q_norm_g, q_scale), cos2, sin2)


def _attn_kernel(q_ref, k_ref, v_ref, lam_ref, sg_ref, o_ref, qs_ref, m_ref, l_ref, a_ref,
                 s0_ref, s1_ref, p0_ref, p1_ref, al0_ref, al1_ref, *, lam_init, tq):
    i = pl.program_id(2)
    vd = a_ref.shape[-1]
    nt = (((1,), (1,)), ((), ()))
    s_refs, p_refs, al_refs = (s0_ref, s1_ref), (p0_ref, p1_ref), (al0_ref, al1_ref)

    lane = lax.broadcasted_iota(jnp.int32, (1, vd), 1)
    is_a = (lane % HEAD_DIM) < HEAD_DIM // 2
    q = q_ref[...]
    zero = jnp.zeros_like(q)
    qs_ref[:tq] = jnp.where(is_a, q, zero)
    qs_ref[tq:] = jnp.where(is_a, zero, q)
    m_ref[...] = jnp.full(m_ref.shape, MASK_VALUE, F32)
    l_ref[...] = jnp.zeros(l_ref.shape, F32)
    a_ref[...] = jnp.zeros(a_ref.shape, F32)

    def scores(blk, slot, masked=False):
        r = pl.multiple_of(blk * tq, tq)
        s = lax.dot_general(qs_ref[...], k_ref[pl.ds(r, tq), :], nt,
                            preferred_element_type=F32)
        if masked:
            row = lax.broadcasted_iota(jnp.int32, (2 * tq, tq), 0) % tq
            col = lax.broadcasted_iota(jnp.int32, (2 * tq, tq), 1)
            s = jnp.where(col <= row, s, MASK_VALUE)
        s_refs[slot][...] = s

    def softmax(slot):
        s = s_refs[slot][...]
        m_old = m_ref[...]
        m_new = jnp.maximum(m_old, jnp.max(s, axis=-1, keepdims=True))
        alpha = jnp.exp2(m_old - m_new)
        p = jnp.exp2(s - jnp.tile(m_new, (1, tq // LANES)))
        l_ref[...] = alpha * l_ref[...] + jnp.sum(p, axis=-1, keepdims=True)
        m_ref[...] = m_new
        p_refs[slot][...] = p.astype(BF16)
        al_refs[slot][...] = alpha

    def accumulate(blk, slot):
        r = pl.multiple_of(blk * tq, tq)
        pv = jnp.dot(p_refs[slot][...], v_ref[pl.ds(r, tq), :], preferred_element_type=F32)
        a_ref[...] = jnp.tile(al_refs[slot][...], (1, vd // LANES)) * a_ref[...] + pv

    def block_at(pos):
        return jnp.where(pos == 0, i, pos - 1)

    def step(t, par):
        accumulate(block_at(t - 2), par)
        softmax(1 - par)
        scores(t - 1, par)

    scores(i, 0, masked=True)

    @pl.when(i == 0)
    def _():
        softmax(0)
        accumulate(i, 0)

    @pl.when(i >= 1)
    def _():
        softmax(0)
        scores(0, 1)

        def pair(u, carry):
            t = 2 + 2 * u
            step(t, 0)
            step(t + 1, 1)
            return carry

        lax.fori_loop(0, (i - 1) // 2, pair, 0)

        @pl.when(i % 2 == 0)
        def _():
            step(i, 0)
            accumulate(block_at(i - 1), 1)
            softmax(0)
            accumulate(block_at(i), 0)

        @pl.when(i % 2 == 1)
        def _():
            accumulate(block_at(i - 1), 0)
            softmax(1)
            accumulate(block_at(i), 1)

    lp = lam_ref[...]
    lam = (jnp.exp(jnp.sum(lp[0:1] * lp[1:2], axis=-1, keepdims=True))
           - jnp.exp(jnp.sum(lp[2:3] * lp[3:4], axis=-1, keepdims=True)) + np.float32(lam_init))
    o1 = a_ref[:tq] / jnp.tile(l_ref[:tq], (1, vd // LANES))
    o2 = a_ref[tq:] / jnp.tile(l_ref[tq:], (1, vd // LANES))
    o = o1 - lam * o2
    ms = jnp.mean(o * o, axis=-1, keepdims=True)
    o = (o * lax.rsqrt(ms + EPS)) * sg_ref[...] * np.float32(1.0 - lam_init)
    o_ref[...] = o.astype(o_ref.dtype)


def _diff_attention(qkv, lam_params, subln_g, lam_init, n_heads, tq=512):
    b, s, d3 = qkv.shape
    d = d3 // 3
    vd = 2 * HEAD_DIM
    return pl.pallas_call(
        functools.partial(_attn_kernel, lam_init=lam_init, tq=tq),
        grid=(b, n_heads, s // tq),
        in_specs=[
            pl.BlockSpec((None, tq, vd), lambda bi, h, i: (bi, i, 2 * n_heads + h)),
            pl.BlockSpec((None, s, vd), lambda bi, h, i: (bi, 0, h)),
            pl.BlockSpec((None, s, vd), lambda bi, h, i: (bi, 0, n_heads + h)),
            pl.BlockSpec((4, HEAD_DIM), lambda bi, h, i: (0, 0)),
            pl.BlockSpec((1, vd), lambda bi, h, i: (0, 0)),
        ],
        out_specs=pl.BlockSpec((None, tq, vd), lambda bi, h, i: (bi, i, h)),
        out_shape=jax.ShapeDtypeStruct((b, s, d), BF16),
        scratch_shapes=[pltpu.VMEM((2 * tq, vd), BF16),
                        pltpu.VMEM((2 * tq, LANES), F32), pltpu.VMEM((2 * tq, LANES), F32),
                        pltpu.VMEM((2 * tq, vd), F32),
                        pltpu.VMEM((2 * tq, tq), F32), pltpu.VMEM((2 * tq, tq), F32),
                        pltpu.VMEM((2 * tq, tq), BF16), pltpu.VMEM((2 * tq, tq), BF16),
                        pltpu.VMEM((2 * tq, LANES), F32), pltpu.VMEM((2 * tq, LANES), F32)],
        compiler_params=_cparams(3),
        name="diff_attention",
    )(qkv, qkv, qkv, lam_params, subln_g.reshape(1, vd))


def _oproj_kernel(a_ref, w_ref, x_ref, gate_ref, o_ref):
    y = jnp.dot(a_ref[...], w_ref[...].astype(BF16), preferred_element_type=F32)
    o_ref[...] = x_ref[...] + gate_ref[...] * y


def _oproj(a, w, x, gate, seq, tm=1024, tn=1024):
    t, d = x.shape
    k = a.shape[1]
    tpb = seq // tm
    return pl.pallas_call(
        _oproj_kernel,
        grid=(d // tn, t // tm),
        in_specs=[
            pl.BlockSpec((tm, k), lambda j, i: (i, 0)),
            pl.BlockSpec((k, tn), lambda j, i: (0, j)),
            pl.BlockSpec((tm, tn), lambda j, i: (i, j)),
            pl.BlockSpec((None, 1, tn), lambda j, i: (i // tpb, 0, j)),
        ],
        out_specs=pl.BlockSpec((tm, tn), lambda j, i: (i, j)),
        out_shape=jax.ShapeDtypeStruct((t, d), F32),
        compiler_params=_cparams(2),
        name="out_proj",
    )(a, w, x, gate)


def kernel(x, c, positions, a_ada_w, a_ada_b, a_norm1_g, a_in_w, a_in_b, a_sgu_ln_g, a_sgu_ln_b, a_sgu_w, a_sgu_b, a_out_w, a_norm2_g, a_ffn_wi, a_ffn_wo, kv_ada_w, kv_ada_b, kv_norm_g, kv_w, k_norm_g, b_ada_w, b_ada_b, b_norm1_g, b_q_w, b_q_norm_g, b_lambda_q1, b_lambda_k1, b_lambda_q2, b_lambda_k2, b_subln_g, b_o_w, b_norm2_g, b_ffn_wi, b_ffn_wo):
    bsz, seq, d = x.shape
    n_a = a_ada_w.shape[0]
    n_b = b_ada_w.shape[0]
    n_heads = d // (2 * HEAD_DIM)
    t = bsz * seq
    q_scale = HEAD_DIM ** -0.5 * math.log2(math.e)

    cos2, sin2 = _rope_tables(positions)
    xf = x.reshape(t, d)

    for l in range(n_a):
        sh1, sc1, g1, sh2, sc2, g2 = _ada_mod(c, a_ada_w[l], a_ada_b[l], 6)
        xf = _gmlp_mix(xf, sh1, sc1, g1, a_norm1_g[l], a_in_w[l], a_in_b[l], a_sgu_ln_g[l],
                       a_sgu_ln_b[l], a_sgu_w[l], a_sgu_b[l], a_out_w[l], seq)
        xf = _ffn(xf, sh2, sc2, g2, a_norm2_g[l], a_ffn_wi[l], a_ffn_wo[l], seq)

    assert n_b == 1, "only one differential-attention layer is supported"
    shk, sck = _ada_mod(c, kv_ada_w, kv_ada_b, 2)

    for jl in range(n_b):
        layer_idx = n_a + jl
        lam_init = 0.8 - 0.6 * math.exp(-0.3 * layer_idx)
        sh1, sc1, g1, sh2, sc2, g2 = _ada_mod(c, b_ada_w[jl], b_ada_b[jl], 6)
        qkv = _qkv_proj(xf, shk, sck, kv_norm_g, sh1, sc1, b_norm1_g[jl], kv_w, b_q_w[jl],
                        k_norm_g, b_q_norm_g[jl], cos2, sin2, seq, q_scale)
        lam_params = jnp.stack([b_lambda_q1[jl], b_lambda_k1[jl],
                                b_lambda_q2[jl], b_lambda_k2[jl]])
        o = _diff_attention(qkv.reshape(bsz, seq, 3 * d), lam_params, b_subln_g[jl], lam_init,
                            n_heads)
        xf = _oproj(o.reshape(t, d), b_o_w[jl], xf, g1, seq)
        xf = _ffn(xf, sh2, sc2, g2, b_norm2_g[jl], b_ffn_wi[jl], b_ffn_wo[jl], seq, tf=512)

    return xf.reshape(bsz, seq, d)
```

```python
import functools
import math

import jax
import jax.numpy as jnp
import numpy as np
from jax import lax
from jax.experimental import pallas as pl
from jax.experimental.pallas import tpu as pltpu

F32 = jnp.float32
BF16 = jnp.bfloat16

EPS = 1e-6
ROPE_THETA = 10000.0
HEAD_DIM = 128
CHUNK = 128
LANES = 128
SUBLANES = 8
MXU_COLS = 256
VMEM_LIMIT_BYTES = 58 * 1024 * 1024
MASK_VALUE = -1e30
NORM_ROWS = 128
ROW_PIECES = 4


def _cparams(n_axes):
    return pltpu.CompilerParams(
        dimension_semantics=("arbitrary",) * n_axes,
        vmem_limit_bytes=VMEM_LIMIT_BYTES,
    )


def _sigmoid(x):
    return 1.0 / (1.0 + jnp.exp(-x))


def _norm_mod_store(x_ref, variants):
    tm = x_ref.shape[0]
    for g_ref, sh_ref, sc_ref, h_ref in variants:
        gain = g_ref[...] * (1.0 + sc_ref[...])
        shift = sh_ref[...]

        def body(c, carry, gain=gain, shift=shift, h_ref=h_ref):
            r = pl.multiple_of(c * NORM_ROWS, NORM_ROWS)
            x = x_ref[pl.ds(r, NORM_ROWS), :]
            y = x * lax.rsqrt(jnp.mean(x * x, axis=-1, keepdims=True) + EPS)
            h_ref[pl.ds(r, NORM_ROWS), :] = (y * gain + shift).astype(BF16)
            return carry

        lax.fori_loop(0, tm // NORM_ROWS, body, 0)


def _rope_kernel(pos_ref, freq_ref, cos_ref, sin_ref):
    ang = pos_ref[...].astype(F32) * freq_ref[...]
    cos_ref[...] = jnp.cos(ang)
    sin_ref[...] = jnp.sin(ang)


def _rope_tables(positions, tm=1024):
    t = positions.size
    inv_freq = 1.0 / (ROPE_THETA ** (jnp.arange(0, HEAD_DIM, 2, dtype=F32) / HEAD_DIM))
    tab = jnp.concatenate([inv_freq, inv_freq]).reshape(1, HEAD_DIM)
    pos = positions.reshape(t, 1)
    return pl.pallas_call(
        _rope_kernel,
        grid=(t // tm,),
        in_specs=[pl.BlockSpec((tm, 1), lambda i: (i, 0)),
                  pl.BlockSpec((1, HEAD_DIM), lambda i: (0, 0))],
        out_specs=[pl.BlockSpec((tm, HEAD_DIM), lambda i: (i, 0)),
                   pl.BlockSpec((tm, HEAD_DIM), lambda i: (i, 0))],
        out_shape=[jax.ShapeDtypeStruct((t, HEAD_DIM), F32)] * 2,
        compiler_params=_cparams(1),
        name="rope_tables",
    )(pos, tab)


def _ada_kernel(c_ref, w_ref, b_ref, o_ref, s_ref, *, row_chunk):
    nb, k, _ = c_ref.shape
    tn = w_ref.shape[1]
    n_chunks = k // row_chunk

    @pl.when(pl.program_id(0) == 0)
    def _():
        for b in range(nb):
            cb = c_ref[b]
            s_ref[b] = jnp.broadcast_to(cb * _sigmoid(cb), (k, LANES))

    def body(kk, accs):
        r = pl.multiple_of(kk * row_chunk, row_chunk)
        w = w_ref[pl.ds(r, row_chunk), :]
        new = []
        for b in range(nb):
            sb = jnp.tile(s_ref[b, pl.ds(r, row_chunk), :], (1, tn // LANES))
            prod = (w * sb).reshape(row_chunk // SUBLANES, SUBLANES, tn)
            new.append(accs[b] + jnp.sum(prod, axis=0))
        return tuple(new)

    accs = lax.fori_loop(0, n_chunks, body,
                         tuple(jnp.zeros((SUBLANES, tn), F32) for _ in range(nb)))
    for b in range(nb):
        o_ref[b:b + 1, :] = jnp.sum(accs[b], axis=0, keepdims=True) + b_ref[...]


def _ada_mod(c, w, b, n, tn=1024, row_chunk=64):
    nb, k = c.shape
    nd = w.shape[1]
    out = pl.pallas_call(
        functools.partial(_ada_kernel, row_chunk=row_chunk),
        grid=(nd // tn,),
        in_specs=[pl.BlockSpec((nb, k, 1), lambda j: (0, 0, 0)),
                  pl.BlockSpec((k, tn), lambda j: (0, j)),
                  pl.BlockSpec((1, tn), lambda j: (0, j))],
        out_specs=pl.BlockSpec((nb, tn), lambda j: (0, j)),
        out_shape=jax.ShapeDtypeStruct((nb, nd), F32),
        scratch_shapes=[pltpu.VMEM((nb, k, LANES), F32)],
        compiler_params=_cparams(1),
        name="ada_mod",
    )(c.reshape(nb, k, 1), w, b.reshape(1, nd))
    d = nd // n
    return [out[:, i * d:(i + 1) * d].reshape(nb, 1, d) for i in range(n)]


def _gelu(z):
    return 0.5 * z * (1.0 + lax.erf(z * np.float32(math.sqrt(0.5))))


def _gmlp_kernel(x_ref, sh_ref, sc_ref, gate_ref, ng_ref, inw_ref, inb_ref, lng_ref, lnb_ref,
                 sguw_ref, sgub_ref, outw_ref, o_ref, h_ref, sv_ref, p_ref, wm_ref,
                 *, nv, tn1, tn2):
    j = pl.program_id(1)
    tm, d = x_ref.shape
    n_groups = sguw_ref.shape[0]

    @pl.when(j == 0)
    def _():
        _norm_mod_store(x_ref, [(ng_ref, sh_ref, sc_ref, h_ref)])

    def in_proj():
        z = jnp.dot(h_ref[...], inw_ref[...].astype(BF16), preferred_element_type=F32)
        return _gelu(z + inb_ref[...])

    @pl.when(j < nv)
    def _():
        sv_ref[:, pl.ds(pl.multiple_of(j * tn1, tn1), tn1)] = in_proj()

    @pl.when(j == nv)
    def _():
        row = lax.broadcasted_iota(jnp.int32, (CHUNK, CHUNK), 0)
        col = lax.broadcasted_iota(jnp.int32, (CHUNK, CHUNK), 1)
        for g in range(n_groups):
            wm_ref[g] = jnp.where(col <= row, sguw_ref[g], 0.0).astype(BF16)

        def chunk_body(cidx, carry):
            r = pl.multiple_of(cidx * CHUNK, CHUNK)
            v = sv_ref[pl.ds(r, CHUNK), :]
            mu = jnp.mean(v, axis=-1, keepdims=True)
            vc = v - mu
            var = jnp.mean(vc * vc, axis=-1, keepdims=True)
            vn = ((vc * lax.rsqrt(var + EPS)) * lng_ref[...] + lnb_ref[...]).astype(BF16)
            for g in range(n_groups):
                lo = g * CHUNK
                sv = jnp.dot(wm_ref[g], vn[:, lo:lo + CHUNK], preferred_element_type=F32)
                sv_ref[pl.ds(r, CHUNK), lo:lo + CHUNK] = sv + sgub_ref[:, lo:lo + CHUNK]
            return carry

        lax.fori_loop(0, tm // CHUNK, chunk_body, 0)

    @pl.when((j >= nv) & (j < 2 * nv))
    def _():
        c0 = pl.multiple_of((j - nv) * tn1, tn1)
        p_ref[:, pl.ds(c0, tn1)] = (in_proj() * sv_ref[:, pl.ds(c0, tn1)]).astype(BF16)

    @pl.when(j >= 2 * nv)
    def _():
        y = jnp.dot(p_ref[...], outw_ref[...].astype(BF16), preferred_element_type=F32)
        c0 = pl.multiple_of((j - 2 * nv) * tn2, tn2)
        o_ref[...] = x_ref[:, pl.ds(c0, tn2)] + gate_ref[:, pl.ds(c0, tn2)] * y


def _gmlp_mix(x, sh, sc, gate, norm_g, in_w, in_b, ln_g, ln_b, sgu_w, sgu_b, out_w, seq,
              tm=1024, tn1=1024, tn2=512):
    t, d = x.shape
    nv = d // tn1
    n2 = d // tn2
    tpb = seq // tm
    n_groups = sgu_w.shape[0]
    sgub_full = jnp.repeat(sgu_b.T, d // n_groups, axis=1)
    vec = lambda: pl.BlockSpec((None, 1, d), lambda i, j: (i // tpb, 0, 0))
    row = lambda n: pl.BlockSpec((1, n), lambda i, j: (0, 0))

    def in_col(i, j):
        jj = jnp.minimum(j, 2 * nv - 1)
        return (0, jnp.where(jj < nv, jj + nv, jj - nv))

    out_col = lambda i, j: jnp.maximum(j - 2 * nv, 0)
    return pl.pallas_call(
        functools.partial(_gmlp_kernel, nv=nv, tn1=tn1, tn2=tn2),
        grid=(t // tm, 2 * nv + n2),
        in_specs=[
            pl.BlockSpec((tm, d), lambda i, j: (i, 0), pipeline_mode=pl.Buffered(1)),
            vec(), vec(), vec(), row(d),
            pl.BlockSpec((d, tn1), in_col),
            pl.BlockSpec((1, tn1), in_col),
            row(d), row(d),
            pl.BlockSpec((n_groups, CHUNK, CHUNK), lambda i, j: (0, 0, 0)),
            pl.BlockSpec((CHUNK, d), lambda i, j: (0, 0)),
            pl.BlockSpec((d, tn2), lambda i, j: (0, out_col(i, j))),
        ],
        out_specs=pl.BlockSpec((tm, tn2), lambda i, j: (i, out_col(i, j))),
        out_shape=jax.ShapeDtypeStruct((t, d), F32),
        scratch_shapes=[pltpu.VMEM((tm, d), BF16),
                        pltpu.VMEM((tm, d), F32),
                        pltpu.VMEM((tm, d), BF16),
                        pltpu.VMEM((n_groups, CHUNK, CHUNK), BF16)],
        compiler_params=_cparams(2),
        name="gmlp_mix",
    )(x, sh, sc, gate, norm_g.reshape(1, d), in_w.astype(BF16), in_b.reshape(1, 2 * d),
      ln_g.reshape(1, d), ln_b.reshape(1, d), sgu_w, sgub_full, out_w.astype(BF16))


def _ffn_kernel(x_ref, sh_ref, sc_ref, gate_ref, ng_ref, wg_ref, wu_ref, wo_ref, o_ref, h_ref):
    j = pl.program_id(1)

    @pl.when(j == 0)
    def _():
        _norm_mod_store(x_ref, [(ng_ref, sh_ref, sc_ref, h_ref)])
        o_ref[...] = jnp.zeros(o_ref.shape, F32)

    h = h_ref[...]
    g = jnp.dot(h, wg_ref[...].astype(BF16), preferred_element_type=F32)
    u = jnp.dot(h, wu_ref[...].astype(BF16), preferred_element_type=F32)
    a = (g * _sigmoid(g) * u).astype(BF16)
    o_ref[...] += jnp.dot(a, wo_ref[...].astype(BF16), preferred_element_type=F32)

    @pl.when(j == pl.num_programs(1) - 1)
    def _():
        o_ref[...] = x_ref[...] + gate_ref[...] * o_ref[...]


def _ffn(x, sh, sc, gate, norm_g, wi, wo, seq, tm=1024, tf=512):
    t, d = x.shape
    f = wo.shape[0]
    nf = f // tf
    tpb = seq // tm
    vec = lambda: pl.BlockSpec((None, 1, d), lambda i, j: (i // tpb, 0, 0))
    return pl.pallas_call(
        _ffn_kernel,
        grid=(t // tm, nf),
        in_specs=[
            pl.BlockSpec((tm, d), lambda i, j: (i, 0), pipeline_mode=pl.Buffered(1)),
            vec(), vec(), vec(),
            pl.BlockSpec((1, d), lambda i, j: (0, 0)),
            pl.BlockSpec((d, tf), lambda i, j: (0, j)),
            pl.BlockSpec((d, tf), lambda i, j: (0, j + nf)),
            pl.BlockSpec((tf, d), lambda i, j: (j, 0)),
        ],
        out_specs=pl.BlockSpec((tm, d), lambda i, j: (i, 0)),
        out_shape=jax.ShapeDtypeStruct((t, d), F32),
        scratch_shapes=[pltpu.VMEM((tm, d), BF16)],
        compiler_params=_cparams(2),
        name="swiglu_ffn",
    )(x, sh, sc, gate, norm_g.reshape(1, d), wi, wi, wo)


def _pair_split_kernel(w_ref, o_ref):
    half = HEAD_DIM // 2
    first = lax.broadcasted_iota(jnp.int32, (1, HEAD_DIM), 1) < half
    for pr in range(w_ref.shape[1] // MXU_COLS):
        c0 = pr * MXU_COLS
        a = w_ref[:, c0:c0 + HEAD_DIM]
        b = w_ref[:, c0 + HEAD_DIM:c0 + MXU_COLS]
        lo = jnp.where(first, a, pltpu.roll(b, half, axis=1))
        hi = jnp.where(first, pltpu.roll(a, half, axis=1), b)
        o_ref[:, c0:c0 + HEAD_DIM] = lo.astype(o_ref.dtype)
        o_ref[:, c0 + HEAD_DIM:c0 + MXU_COLS] = hi.astype(o_ref.dtype)


def _pair_split_cols(w, n, tr=256):
    d = w.shape[0]
    return pl.pallas_call(
        _pair_split_kernel,
        grid=(d // tr,),
        in_specs=[pl.BlockSpec((tr, n), lambda i: (i, 0))],
        out_specs=pl.BlockSpec((tr, n), lambda i: (i, 0)),
        out_shape=jax.ShapeDtypeStruct((d, n), BF16),
        compiler_params=_cparams(1),
        name="pair_split_cols",
    )(w)


def _pair_split_gain(g, scale):
    half = HEAD_DIM // 2
    g = g * np.float32(scale)
    return jnp.stack([jnp.tile(g[:half], 2), jnp.tile(g[half:], 2)])


def _qkv_kernel(x_ref, shk_ref, sck_ref, ngk_ref, shq_ref, scq_ref, ngq_ref, wk_ref, wv_ref,
                wq_ref, hgk_ref, hgq_ref, cos_ref, sin_ref, o_ref, hkv_ref, hq_ref,
                *, nk, nv):
    j = pl.program_id(1)
    tn = o_ref.shape[1]

    @pl.when(j == 0)
    def _():
        _norm_mod_store(x_ref, [(ngk_ref, shk_ref, sck_ref, hkv_ref),
                                (ngq_ref, shq_ref, scq_ref, hq_ref)])

    tm = o_ref.shape[0]
    tr = tm // ROW_PIECES

    def project_rope(h_ref, w_ref, hg_ref):
        g_lo = hg_ref[0:1, :]
        g_hi = hg_ref[1:2, :]
        is_a = lax.broadcasted_iota(jnp.int32, (1, HEAD_DIM), 1) < HEAD_DIM // 2
        inv_n = np.float32(1.0 / HEAD_DIM)
        for rp in range(ROW_PIECES):
            rows = slice(rp * tr, (rp + 1) * tr)
            cos = cos_ref[rows, :]
            sin = sin_ref[rows, :]
            t_ll, t_lh = cos * g_lo, -(sin * g_hi)
            t_hh, t_hl = cos * g_hi, sin * g_lo
            for c in range(tn // MXU_COLS):
                c0 = c * MXU_COLS
                y = jnp.dot(h_ref[rows, :], w_ref[:, c0:c0 + MXU_COLS],
                            preferred_element_type=F32)
                y_lo = y[:, :HEAD_DIM]
                y_hi = y[:, HEAD_DIM:]
                sq = y_lo * y_lo + y_hi * y_hi
                sum_a = jnp.sum(jnp.where(is_a, sq, 0.0), axis=-1, keepdims=True)
                sum_b = jnp.sum(sq, axis=-1, keepdims=True) - sum_a
                rs = jnp.where(is_a, lax.rsqrt(sum_a * inv_n + EPS),
                               lax.rsqrt(sum_b * inv_n + EPS))
                o_ref[rows, c0:c0 + HEAD_DIM] = (
                    (y_lo * t_ll + y_hi * t_lh) * rs).astype(o_ref.dtype)
                o_ref[rows, c0 + HEAD_DIM:c0 + MXU_COLS] = (
                    (y_hi * t_hh + y_lo * t_hl) * rs).astype(o_ref.dtype)

    @pl.when(j < nk)
    def _():
        project_rope(hkv_ref, wk_ref, hgk_ref)

    @pl.when((j >= nk) & (j < nk + nv))
    def _():
        for c in range(tn // MXU_COLS):
            c0 = c * MXU_COLS
            y = jnp.dot(hkv_ref[...], wv_ref[:, c0:c0 + MXU_COLS].astype(BF16),
                        preferred_element_type=F32)
            o_ref[:, c0:c0 + MXU_COLS] = y.astype(o_ref.dtype)

    @pl.when(j >= nk + nv)
    def _():
        project_rope(hq_ref, wq_ref, hgq_ref)


def _qkv_proj(x, shk, sck, ngk, shq, scq, ngq, kv_w, q_w, k_norm_g, q_norm_g, cos2, sin2, seq,
              q_scale, tm=1024, tn=512):
    t, d = x.shape
    nk = d // tn
    nv = (kv_w.shape[1] - d) // tn
    nq = q_w.shape[1] // tn
    tpb = seq // tm
    wk = _pair_split_cols(kv_w, d)
    wq = _pair_split_cols(q_w, q_w.shape[1])
    vec = lambda: pl.BlockSpec((None, 1, d), lambda i, j: (i // tpb, 0, 0))
    row = lambda n: pl.BlockSpec((1, n), lambda i, j: (0, 0))
    gain = lambda: pl.BlockSpec((2, HEAD_DIM), lambda i, j: (0, 0))
    tab = lambda: pl.BlockSpec((tm, HEAD_DIM), lambda i, j: (i, 0))
    return pl.pallas_call(
        functools.partial(_qkv_kernel, nk=nk, nv=nv),
        grid=(t // tm, nk + nv + nq),
        in_specs=[
            pl.BlockSpec((tm, d), lambda i, j: (i, 0)),
            vec(), vec(), row(d), vec(), vec(), row(d),
            pl.BlockSpec((d, tn), lambda i, j: (0, jnp.minimum(j, nk - 1))),
            pl.BlockSpec((d, tn), lambda i, j: (0, jnp.clip(j, nk, nk + nv - 1))),
            pl.BlockSpec((d, tn), lambda i, j: (0, jnp.maximum(j - nk - nv, 0))),
            gain(), gain(), tab(), tab(),
        ],
        out_specs=pl.BlockSpec((tm, tn), lambda i, j: (i, j)),
        out_shape=jax.ShapeDtypeStruct((t, kv_w.shape[1] + q_w.shape[1]), BF16),
        scratch_shapes=[pltpu.VMEM((tm, d), BF16), pltpu.VMEM((tm, d), BF16)],
        compiler_params=_cparams(2),
        name="qkv_proj",
    )(x, shk, sck, ngk.reshape(1, d), shq, scq, ngq.reshape(1, d), wk, kv_w, wq,
      _pair_split_gain(k_norm_g, 1.0), _pair_split_gain(q_norm_g, q_scale), cos2, sin2)


def _attn_kernel(q_ref, k_ref, v_ref, lam_ref, sg_ref, o_ref, qs_ref, m_ref, l_ref, a_ref,
                 s0_ref, s1_ref, p0_ref, p1_ref, al0_ref, al1_ref, *, lam_init, tq):
    i = pl.program_id(2)
    vd = a_ref.shape[-1]
    nt = (((1,), (1,)), ((), ()))
    s_refs, p_refs, al_refs = (s0_ref, s1_ref), (p0_ref, p1_ref), (al0_ref, al1_ref)

    lane = lax.broadcasted_iota(jnp.int32, (1, vd), 1)
    is_a = (lane % HEAD_DIM) < HEAD_DIM // 2
    q = q_ref[...]
    zero = jnp.zeros_like(q)
    qs_ref[:tq] = jnp.where(is_a, q, zero)
    qs_ref[tq:] = jnp.where(is_a, zero, q)
    m_ref[...] = jnp.full(m_ref.shape, MASK_VALUE, F32)
    l_ref[...] = jnp.zeros(l_ref.shape, F32)
    a_ref[...] = jnp.zeros(a_ref.shape, F32)

    def scores(blk, slot, masked=False):
        r = pl.multiple_of(blk * tq, tq)
        s = lax.dot_general(qs_ref[...], k_ref[pl.ds(r, tq), :], nt,
                            preferred_element_type=F32)
        if masked:
            row = lax.broadcasted_iota(jnp.int32, (2 * tq, tq), 0) % tq
            col = lax.broadcasted_iota(jnp.int32, (2 * tq, tq), 1)
            s = jnp.where(col <= row, s, MASK_VALUE)
        s_refs[slot][...] = s

    def softmax(slot):
        s = s_refs[slot][...]
        m_old = m_ref[...]
        m_new = jnp.maximum(m_old, jnp.max(s, axis=-1, keepdims=True))
        alpha = jnp.exp2(m_old - m_new)
        p = jnp.exp2(s - jnp.tile(m_new, (1, tq // LANES)))
        l_ref[...] = alpha * l_ref[...] + jnp.sum(p, axis=-1, keepdims=True)
        m_ref[...] = m_new
        p_refs[slot][...] = p.astype(BF16)
        al_refs[slot][...] = alpha

    def accumulate(blk, slot):
        r = pl.multiple_of(blk * tq, tq)
        pv = jnp.dot(p_refs[slot][...], v_ref[pl.ds(r, tq), :], preferred_element_type=F32)
        a_ref[...] = jnp.tile(al_refs[slot][...], (1, vd // LANES)) * a_ref[...] + pv

    def block_at(pos):
        return jnp.where(pos == 0, i, pos - 1)

    def step(t, par):
        accumulate(block_at(t - 2), par)
        softmax(1 - par)
        scores(t - 1, par)

    scores(i, 0, masked=True)

    @pl.when(i == 0)
    def _():
        softmax(0)
        accumulate(i, 0)

    @pl.when(i >= 1)
    def _():
        softmax(0)
        scores(0, 1)

        def pair(u, carry):
            t = 2 + 2 * u
            step(t, 0)
            step(t + 1, 1)
            return carry

        lax.fori_loop(0, (i - 1) // 2, pair, 0)

        @pl.when(i % 2 == 0)
        def _():
            step(i, 0)
            accumulate(block_at(i - 1), 1)
            softmax(0)
            accumulate(block_at(i), 0)

        @pl.when(i % 2 == 1)
        def _():
            accumulate(block_at(i - 1), 0)
            softmax(1)
            accumulate(block_at(i), 1)

    lp = lam_ref[...]
    lam = (jnp.exp(jnp.sum(lp[0:1] * lp[1:2], axis=-1, keepdims=True))
           - jnp.exp(jnp.sum(lp[2:3] * lp[3:4], axis=-1, keepdims=True)) + np.float32(lam_init))
    o1 = a_ref[:tq] / jnp.tile(l_ref[:tq], (1, vd // LANES))
    o2 = a_ref[tq:] / jnp.tile(l_ref[tq:], (1, vd // LANES))
    o = o1 - lam * o2
    ms = jnp.mean(o * o, axis=-1, keepdims=True)
    o = (o * lax.rsqrt(ms + EPS)) * sg_ref[...] * np.float32(1.0 - lam_init)
    o_ref[...] = o.astype(o_ref.dtype)


def _diff_attention(qkv, lam_params, subln_g, lam_init, n_heads, tq=512):
    b, s, d3 = qkv.shape
    d = d3 // 3
    vd = 2 * HEAD_DIM
    return pl.pallas_call(
        functools.partial(_attn_kernel, lam_init=lam_init, tq=tq),
        grid=(b, n_heads, s // tq),
        in_specs=[
            pl.BlockSpec((None, tq, vd), lambda bi, h, i: (bi, i, 2 * n_heads + h)),
            pl.BlockSpec((None, s, vd), lambda bi, h, i: (bi, 0, h)),
            pl.BlockSpec((None, s, vd), lambda bi, h, i: (bi, 0, n_heads + h)),
            pl.BlockSpec((4, HEAD_DIM), lambda bi, h, i: (0, 0)),
            pl.BlockSpec((1, vd), lambda bi, h, i: (0, 0)),
        ],
        out_specs=pl.BlockSpec((None, tq, vd), lambda bi, h, i: (bi, i, h)),
        out_shape=jax.ShapeDtypeStruct((b, s, d), BF16),
        scratch_shapes=[pltpu.VMEM((2 * tq, vd), BF16),
                        pltpu.VMEM((2 * tq, LANES), F32), pltpu.VMEM((2 * tq, LANES), F32),
                        pltpu.VMEM((2 * tq, vd), F32),
                        pltpu.VMEM((2 * tq, tq), F32), pltpu.VMEM((2 * tq, tq), F32),
                        pltpu.VMEM((2 * tq, tq), BF16), pltpu.VMEM((2 * tq, tq), BF16),
                        pltpu.VMEM((2 * tq, LANES), F32), pltpu.VMEM((2 * tq, LANES), F32)],
        compiler_params=_cparams(3),
        name="diff_attention",
    )(qkv, qkv, qkv, lam_params, subln_g.reshape(1, vd))


def _oproj_kernel(a_ref, w_ref, x_ref, gate_ref, o_ref):
    y = jnp.dot(a_ref[...], w_ref[...].astype(BF16), preferred_element_type=F32)
    o_ref[...] = x_ref[...] + gate_ref[...] * y


def _oproj(a, w, x, gate, seq, tm=1024, tn=1024):
    t, d = x.shape
    k = a.shape[1]
    tpb = seq // tm
    return pl.pallas_call(
        _oproj_kernel,
        grid=(d // tn, t // tm),
        in_specs=[
            pl.BlockSpec((tm, k), lambda j, i: (i, 0)),
            pl.BlockSpec((k, tn), lambda j, i: (0, j)),
            pl.BlockSpec((tm, tn), lambda j, i: (i, j)),
            pl.BlockSpec((None, 1, tn), lambda j, i: (i // tpb, 0, j)),
        ],
        out_specs=pl.BlockSpec((tm, tn), lambda j, i: (i, j)),
        out_shape=jax.ShapeDtypeStruct((t, d), F32),
        compiler_params=_cparams(2),
        name="out_proj",
    )(a, w, x, gate)


def kernel(x, c, positions, a_ada_w, a_ada_b, a_norm1_g, a_in_w, a_in_b, a_sgu_ln_g, a_sgu_ln_b, a_sgu_w, a_sgu_b, a_out_w, a_norm2_g, a_ffn_wi, a_ffn_wo, kv_ada_w, kv_ada_b, kv_norm_g, kv_w, k_norm_g, b_ada_w, b_ada_b, b_norm1_g, b_q_w, b_q_norm_g, b_lambda_q1, b_lambda_k1, b_lambda_q2, b_lambda_k2, b_subln_g, b_o_w, b_norm2_g, b_ffn_wi, b_ffn_wo):
    bsz, seq, d = x.shape
    n_a = a_ada_w.shape[0]
    n_b = b_ada_w.shape[0]
    n_heads = d // (2 * HEAD_DIM)
    t = bsz * seq
    q_scale = HEAD_DIM ** -0.5 * math.log2(math.e)

    cos2, sin2 = _rope_tables(positions)
    xf = x.reshape(t, d)

    for l in range(n_a):
        sh1, sc1, g1, sh2, sc2, g2 = _ada_mod(c, a_ada_w[l], a_ada_b[l], 6)
        xf = _gmlp_mix(xf, sh1, sc1, g1, a_norm1_g[l], a_in_w[l], a_in_b[l], a_sgu_ln_g[l],
                       a_sgu_ln_b[l], a_sgu_w[l], a_sgu_b[l], a_out_w[l], seq)
        xf = _ffn(xf, sh2, sc2, g2, a_norm2_g[l], a_ffn_wi[l], a_ffn_wo[l], seq)

    assert n_b == 1, "only one differential-attention layer is supported"
    shk, sck = _ada_mod(c, kv_ada_w, kv_ada_b, 2)

    for jl in range(n_b):
        layer_idx = n_a + jl
        lam_init = 0.8 - 0.6 * math.exp(-0.3 * layer_idx)
        sh1, sc1, g1, sh2, sc2, g2 = _ada_mod(c, b_ada_w[jl], b_ada_b[jl], 6)
        qkv = _qkv_proj(xf, shk, sck, kv_norm_g, sh1, sc1, b_norm1_g[jl], kv_w, b_q_w[jl],
                        k_norm_g, b_q_norm_g[jl], cos2, sin2, seq, q_scale)
        lam_params = jnp.stack([b_lambda_q1[jl], b_lambda_k1[jl],
                                b_lambda_q2[jl], b_lambda_k2[jl]])
        o = _diff_attention(qkv.reshape(bsz, seq, 3 * d), lam_params, b_subln_g[jl], lam_init,
                            n_heads)
        xf = _oproj(o.reshape(t, d), b_o_w[jl], xf, g1, seq)
        xf = _ffn(xf, sh2, sc2, g2, b_norm2_g[jl], b_ffn_wi[jl], b_ffn_wo[jl], seq)

    return xf.reshape(bsz, seq, d)
```

```python
import functools
import math

import jax
import jax.numpy as jnp
import numpy as np
from jax import lax
from jax.experimental import pallas as pl
from jax.experimental.pallas import tpu as pltpu

F32 = jnp.float32
BF16 = jnp.bfloat16

EPS = 1e-6
ROPE_THETA = 10000.0
HEAD_DIM = 128
CHUNK = 128
LANES = 128
SUBLANES = 8
MXU_COLS = 256
VMEM_LIMIT_BYTES = 58 * 1024 * 1024
MASK_VALUE = -1e30
NORM_ROWS = 128
ROW_PIECES = 4


def _cparams(n_axes):
    return pltpu.CompilerParams(
        dimension_semantics=("arbitrary",) * n_axes,
        vmem_limit_bytes=VMEM_LIMIT_BYTES,
    )


def _sigmoid(x):
    return 1.0 / (1.0 + jnp.exp(-x))


def _norm_mod_rows(x_ref, g_ref, sh_ref, sc_ref, h_ref, r0, n_rows):
    gain = g_ref[...] * (1.0 + sc_ref[...])
    shift = sh_ref[...]
    for c in range(n_rows // NORM_ROWS):
        rows = pl.ds(r0 + c * NORM_ROWS, NORM_ROWS)
        x = x_ref[rows, :]
        y = x * lax.rsqrt(jnp.mean(x * x, axis=-1, keepdims=True) + EPS)
        h_ref[rows, :] = (y * gain + shift).astype(BF16)


def _norm_mod_store(x_ref, variants):
    tm = x_ref.shape[0]
    for g_ref, sh_ref, sc_ref, h_ref in variants:

        def body(c, carry, refs=(g_ref, sh_ref, sc_ref, h_ref)):
            _norm_mod_rows(x_ref, *refs, pl.multiple_of(c * NORM_ROWS, NORM_ROWS), NORM_ROWS)
            return carry

        lax.fori_loop(0, tm // NORM_ROWS, body, 0)


def _rope_kernel(pos_ref, freq_ref, cos_ref, sin_ref):
    ang = pos_ref[...].astype(F32) * freq_ref[...]
    cos_ref[...] = jnp.cos(ang)
    sin_ref[...] = jnp.sin(ang)


def _rope_tables(positions, tm=1024):
    t = positions.size
    inv_freq = 1.0 / (ROPE_THETA ** (jnp.arange(0, HEAD_DIM, 2, dtype=F32) / HEAD_DIM))
    tab = jnp.concatenate([inv_freq, inv_freq]).reshape(1, HEAD_DIM)
    pos = positions.reshape(t, 1)
    return pl.pallas_call(
        _rope_kernel,
        grid=(t // tm,),
        in_specs=[pl.BlockSpec((tm, 1), lambda i: (i, 0)),
                  pl.BlockSpec((1, HEAD_DIM), lambda i: (0, 0))],
        out_specs=[pl.BlockSpec((tm, HEAD_DIM), lambda i: (i, 0)),
                   pl.BlockSpec((tm, HEAD_DIM), lambda i: (i, 0))],
        out_shape=[jax.ShapeDtypeStruct((t, HEAD_DIM), F32)] * 2,
        compiler_params=_cparams(1),
        name="rope_tables",
    )(pos, tab)


def _ada_kernel(c_ref, w_ref, b_ref, o_ref, s_ref, *, row_chunk):
    nb, k, _ = c_ref.shape
    tn = w_ref.shape[1]
    n_chunks = k // row_chunk

    @pl.when(pl.program_id(0) == 0)
    def _():
        for b in range(nb):
            cb = c_ref[b]
            s_ref[b] = jnp.broadcast_to(cb * _sigmoid(cb), (k, LANES))

    def body(kk, accs):
        r = pl.multiple_of(kk * row_chunk, row_chunk)
        w = w_ref[pl.ds(r, row_chunk), :]
        new = []
        for b in range(nb):
            sb = jnp.tile(s_ref[b, pl.ds(r, row_chunk), :], (1, tn // LANES))
            prod = (w * sb).reshape(row_chunk // SUBLANES, SUBLANES, tn)
            new.append(accs[b] + jnp.sum(prod, axis=0))
        return tuple(new)

    accs = lax.fori_loop(0, n_chunks, body,
                         tuple(jnp.zeros((SUBLANES, tn), F32) for _ in range(nb)))
    for b in range(nb):
        o_ref[b:b + 1, :] = jnp.sum(accs[b], axis=0, keepdims=True) + b_ref[...]


def _ada_mod(c, w, b, n, tn=1024, row_chunk=64):
    nb, k = c.shape
    nd = w.shape[1]
    out = pl.pallas_call(
        functools.partial(_ada_kernel, row_chunk=row_chunk),
        grid=(nd // tn,),
        in_specs=[pl.BlockSpec((nb, k, 1), lambda j: (0, 0, 0)),
                  pl.BlockSpec((k, tn), lambda j: (0, j)),
                  pl.BlockSpec((1, tn), lambda j: (0, j))],
        out_specs=pl.BlockSpec((nb, tn), lambda j: (0, j)),
        out_shape=jax.ShapeDtypeStruct((nb, nd), F32),
        scratch_shapes=[pltpu.VMEM((nb, k, LANES), F32)],
        compiler_params=_cparams(1),
        name="ada_mod",
    )(c.reshape(nb, k, 1), w, b.reshape(1, nd))
    d = nd // n
    return [out[:, i * d:(i + 1) * d].reshape(nb, 1, d) for i in range(n)]


def _gelu(z):
    return 0.5 * z * (1.0 + lax.erf(z * np.float32(math.sqrt(0.5))))


def _gmlp_kernel(x_ref, sh_ref, sc_ref, gate_ref, ng_ref, inw_ref, inb_ref, lng_ref, lnb_ref,
                 sguw_ref, sgub_ref, outw_ref, o_ref, h_ref, sv_ref, p_ref, wm_ref,
                 *, nv, tn1, tn2):
    j = pl.program_id(1)
    tm, d = x_ref.shape
    n_groups = sguw_ref.shape[0]

    tr = tm // ROW_PIECES
    norm = (ng_ref, sh_ref, sc_ref, h_ref)

    def in_proj(rp):
        rows = slice(rp * tr, (rp + 1) * tr)
        z = jnp.dot(h_ref[rows, :], inw_ref[...].astype(BF16), preferred_element_type=F32)
        return rows, _gelu(z + inb_ref[...])

    def sgu_rows(r0):
        rows = slice(r0, r0 + CHUNK)
        v = sv_ref[rows, :]
        mu = jnp.mean(v, axis=-1, keepdims=True)
        vc = v - mu
        var = jnp.mean(vc * vc, axis=-1, keepdims=True)
        vn = ((vc * lax.rsqrt(var + EPS)) * lng_ref[...] + lnb_ref[...]).astype(BF16)
        for g in range(n_groups):
            lo = g * CHUNK
            sv = jnp.dot(wm_ref[g], vn[:, lo:lo + CHUNK], preferred_element_type=F32)
            sv_ref[rows, lo:lo + CHUNK] = sv + sgub_ref[:, lo:lo + CHUNK]

    @pl.when(j == 0)
    def _():
        for rp in range(ROW_PIECES):
            _norm_mod_rows(x_ref, *norm, rp * tr, tr)
            rows, z = in_proj(rp)
            sv_ref[rows, :tn1] = z

    @pl.when((j > 0) & (j < nv))
    def _():
        c0 = pl.multiple_of(j * tn1, tn1)
        for rp in range(ROW_PIECES):
            rows, z = in_proj(rp)
            sv_ref[rows, pl.ds(c0, tn1)] = z

    @pl.when(j == nv)
    def _():
        row = lax.broadcasted_iota(jnp.int32, (CHUNK, CHUNK), 0)
        col = lax.broadcasted_iota(jnp.int32, (CHUNK, CHUNK), 1)
        for g in range(n_groups):
            wm_ref[g] = jnp.where(col <= row, sguw_ref[g], 0.0).astype(BF16)
        for rp in range(ROW_PIECES):
            rows, z = in_proj(rp)
            for r0 in range(rp * tr, (rp + 1) * tr, CHUNK):
                sgu_rows(r0)
            p_ref[rows, :tn1] = (z * sv_ref[rows, :tn1]).astype(BF16)

    @pl.when((j > nv) & (j < 2 * nv))
    def _():
        c0 = pl.multiple_of((j - nv) * tn1, tn1)
        for rp in range(ROW_PIECES):
            rows, z = in_proj(rp)
            p_ref[rows, pl.ds(c0, tn1)] = (z * sv_ref[rows, pl.ds(c0, tn1)]).astype(BF16)

    @pl.when(j >= 2 * nv)
    def _():
        y = jnp.dot(p_ref[...], outw_ref[...].astype(BF16), preferred_element_type=F32)
        c0 = pl.multiple_of((j - 2 * nv) * tn2, tn2)
        o_ref[...] = x_ref[:, pl.ds(c0, tn2)] + gate_ref[:, pl.ds(c0, tn2)] * y


def _gmlp_mix(x, sh, sc, gate, norm_g, in_w, in_b, ln_g, ln_b, sgu_w, sgu_b, out_w, seq,
              tm=1024, tn1=1024, tn2=512):
    t, d = x.shape
    nv = d // tn1
    n2 = d // tn2
    tpb = seq // tm
    n_groups = sgu_w.shape[0]
    sgub_full = jnp.repeat(sgu_b.T, d // n_groups, axis=1)
    vec = lambda: pl.BlockSpec((None, 1, d), lambda i, j: (i // tpb, 0, 0))
    row = lambda n: pl.BlockSpec((1, n), lambda i, j: (0, 0))

    def in_col(i, j):
        jj = jnp.minimum(j, 2 * nv - 1)
        return (0, jnp.where(jj < nv, jj + nv, jj - nv))

    out_col = lambda i, j: jnp.maximum(j - 2 * nv, 0)
    return pl.pallas_call(
        functools.partial(_gmlp_kernel, nv=nv, tn1=tn1, tn2=tn2),
        grid=(t // tm, 2 * nv + n2),
        in_specs=[
            pl.BlockSpec((tm, d), lambda i, j: (i, 0), pipeline_mode=pl.Buffered(1)),
            vec(), vec(), vec(), row(d),
            pl.BlockSpec((d, tn1), in_col),
            pl.BlockSpec((1, tn1), in_col),
            row(d), row(d),
            pl.BlockSpec((n_groups, CHUNK, CHUNK), lambda i, j: (0, 0, 0)),
            pl.BlockSpec((CHUNK, d), lambda i, j: (0, 0)),
            pl.BlockSpec((d, tn2), lambda i, j: (0, out_col(i, j))),
        ],
        out_specs=pl.BlockSpec((tm, tn2), lambda i, j: (i, out_col(i, j))),
        out_shape=jax.ShapeDtypeStruct((t, d), F32),
        scratch_shapes=[pltpu.VMEM((tm, d), BF16),
                        pltpu.VMEM((tm, d), F32),
                        pltpu.VMEM((tm, d), BF16),
                        pltpu.VMEM((n_groups, CHUNK, CHUNK), BF16)],
        compiler_params=_cparams(2),
        name="gmlp_mix",
    )(x, sh, sc, gate, norm_g.reshape(1, d), in_w.astype(BF16), in_b.reshape(1, 2 * d),
      ln_g.reshape(1, d), ln_b.reshape(1, d), sgu_w, sgub_full, out_w.astype(BF16))


def _ffn_kernel(x_ref, sh_ref, sc_ref, gate_ref, ng_ref, wg_ref, wu_ref, wo_ref, o_ref, h_ref):
    j = pl.program_id(1)

    @pl.when(j == 0)
    def _():
        _norm_mod_store(x_ref, [(ng_ref, sh_ref, sc_ref, h_ref)])
        o_ref[...] = jnp.zeros(o_ref.shape, F32)

    h = h_ref[...]
    g = jnp.dot(h, wg_ref[...].astype(BF16), preferred_element_type=F32)
    u = jnp.dot(h, wu_ref[...].astype(BF16), preferred_element_type=F32)
    a = (g * _sigmoid(g) * u).astype(BF16)
    o_ref[...] += jnp.dot(a, wo_ref[...].astype(BF16), preferred_element_type=F32)

    @pl.when(j == pl.num_programs(1) - 1)
    def _():
        o_ref[...] = x_ref[...] + gate_ref[...] * o_ref[...]


def _ffn(x, sh, sc, gate, norm_g, wi, wo, seq, tm=1024, tf=512):
    t, d = x.shape
    f = wo.shape[0]
    nf = f // tf
    tpb = seq // tm
    vec = lambda: pl.BlockSpec((None, 1, d), lambda i, j: (i // tpb, 0, 0))
    return pl.pallas_call(
        _ffn_kernel,
        grid=(t // tm, nf),
        in_specs=[
            pl.BlockSpec((tm, d), lambda i, j: (i, 0), pipeline_mode=pl.Buffered(1)),
            vec(), vec(), vec(),
            pl.BlockSpec((1, d), lambda i, j: (0, 0)),
            pl.BlockSpec((d, tf), lambda i, j: (0, j)),
            pl.BlockSpec((d, tf), lambda i, j: (0, j + nf)),
            pl.BlockSpec((tf, d), lambda i, j: (j, 0)),
        ],
        out_specs=pl.BlockSpec((tm, d), lambda i, j: (i, 0)),
        out_shape=jax.ShapeDtypeStruct((t, d), F32),
        scratch_shapes=[pltpu.VMEM((tm, d), BF16)],
        compiler_params=_cparams(2),
        name="swiglu_ffn",
    )(x, sh, sc, gate, norm_g.reshape(1, d), wi, wi, wo)


def _pair_split_kernel(w_ref, o_ref):
    half = HEAD_DIM // 2
    first = lax.broadcasted_iota(jnp.int32, (1, HEAD_DIM), 1) < half
    for pr in range(w_ref.shape[1] // MXU_COLS):
        c0 = pr * MXU_COLS
        a = w_ref[:, c0:c0 + HEAD_DIM]
        b = w_ref[:, c0 + HEAD_DIM:c0 + MXU_COLS]
        lo = jnp.where(first, a, pltpu.roll(b, half, axis=1))
        hi = jnp.where(first, pltpu.roll(a, half, axis=1), b)
        o_ref[:, c0:c0 + HEAD_DIM] = lo.astype(o_ref.dtype)
        o_ref[:, c0 + HEAD_DIM:c0 + MXU_COLS] = hi.astype(o_ref.dtype)


def _pair_split_cols(w, n, tr=256):
    d = w.shape[0]
    return pl.pallas_call(
        _pair_split_kernel,
        grid=(d // tr,),
        in_specs=[pl.BlockSpec((tr, n), lambda i: (i, 0))],
        out_specs=pl.BlockSpec((tr, n), lambda i: (i, 0)),
        out_shape=jax.ShapeDtypeStruct((d, n), BF16),
        compiler_params=_cparams(1),
        name="pair_split_cols",
    )(w)


def _pair_split_gain(g, scale):
    half = HEAD_DIM // 2
    g = g * np.float32(scale)
    return jnp.stack([jnp.tile(g[:half], 2), jnp.tile(g[half:], 2)])


def _qkv_kernel(x_ref, shk_ref, sck_ref, ngk_ref, shq_ref, scq_ref, ngq_ref, wk_ref, wv_ref,
                wq_ref, hgk_ref, hgq_ref, cos_ref, sin_ref, o_ref, hkv_ref, hq_ref,
                *, nk, nv):
    j = pl.program_id(1)
    tn = o_ref.shape[1]

    tm = o_ref.shape[0]
    tr = tm // ROW_PIECES
    kv_norm = (ngk_ref, shk_ref, sck_ref, hkv_ref)
    q_norm = (ngq_ref, shq_ref, scq_ref, hq_ref)

    def project_rope(h_ref, w_ref, hg_ref, before_rows=None):
        g_lo = hg_ref[0:1, :]
        g_hi = hg_ref[1:2, :]
        is_a = lax.broadcasted_iota(jnp.int32, (1, HEAD_DIM), 1) < HEAD_DIM // 2
        inv_n = np.float32(1.0 / HEAD_DIM)
        for rp in range(ROW_PIECES):
            if before_rows is not None:
                before_rows(rp * tr, tr)
            rows = slice(rp * tr, (rp + 1) * tr)
            cos = cos_ref[rows, :]
            sin = sin_ref[rows, :]
            t_ll, t_lh = cos * g_lo, -(sin * g_hi)
            t_hh, t_hl = cos * g_hi, sin * g_lo
            for c in range(tn // MXU_COLS):
                c0 = c * MXU_COLS
                y = jnp.dot(h_ref[rows, :], w_ref[:, c0:c0 + MXU_COLS],
                            preferred_element_type=F32)
                y_lo = y[:, :HEAD_DIM]
                y_hi = y[:, HEAD_DIM:]
                sq = y_lo * y_lo + y_hi * y_hi
                sum_a = jnp.sum(jnp.where(is_a, sq, 0.0), axis=-1, keepdims=True)
                sum_b = jnp.sum(sq, axis=-1, keepdims=True) - sum_a
                rs = jnp.where(is_a, lax.rsqrt(sum_a * inv_n + EPS),
                               lax.rsqrt(sum_b * inv_n + EPS))
                o_ref[rows, c0:c0 + HEAD_DIM] = (
                    (y_lo * t_ll + y_hi * t_lh) * rs).astype(o_ref.dtype)
                o_ref[rows, c0 + HEAD_DIM:c0 + MXU_COLS] = (
                    (y_hi * t_hh + y_lo * t_hl) * rs).astype(o_ref.dtype)

    @pl.when(j == 0)
    def _():
        project_rope(hkv_ref, wk_ref, hgk_ref,
                     before_rows=lambda r0, n: _norm_mod_rows(x_ref, *kv_norm, r0, n))

    @pl.when((j > 0) & (j < nk))
    def _():
        project_rope(hkv_ref, wk_ref, hgk_ref)

    @pl.when((j >= nk) & (j < nk + nv))
    def _():
        q_rows = tm // nv
        _norm_mod_rows(x_ref, *q_norm, pl.multiple_of((j - nk) * q_rows, q_rows), q_rows)
        for c in range(tn // MXU_COLS):
            c0 = c * MXU_COLS
            y = jnp.dot(hkv_ref[...], wv_ref[:, c0:c0 + MXU_COLS].astype(BF16),
                        preferred_element_type=F32)
            o_ref[:, c0:c0 + MXU_COLS] = y.astype(o_ref.dtype)

    @pl.when(j >= nk + nv)
    def _():
        project_rope(hq_ref, wq_ref, hgq_ref)


def _qkv_proj(x, shk, sck, ngk, shq, scq, ngq, kv_w, q_w, k_norm_g, q_norm_g, cos2, sin2, seq,
              q_scale, tm=1024, tn=512):
    t, d = x.shape
    nk = d // tn
    nv = (kv_w.shape[1] - d) // tn
    nq = q_w.shape[1] // tn
    tpb = seq // tm
    wk = _pair_split_cols(kv_w, d)
    wq = _pair_split_cols(q_w, q_w.shape[1])
    vec = lambda: pl.BlockSpec((None, 1, d), lambda i, j: (i // tpb, 0, 0))
    row = lambda n: pl.BlockSpec((1, n), lambda i, j: (0, 0))
    gain = lambda: pl.BlockSpec((2, HEAD_DIM), lambda i, j: (0, 0))
    tab = lambda: pl.BlockSpec((tm, HEAD_DIM), lambda i, j: (i, 0))
    return pl.pallas_call(
        functools.partial(_qkv_kernel, nk=nk, nv=nv),
        grid=(t // tm, nk + nv + nq),
        in_specs=[
            pl.BlockSpec((tm, d), lambda i, j: (i, 0)),
            vec(), vec(), row(d), vec(), vec(), row(d),
            pl.BlockSpec((d, tn), lambda i, j: (0, jnp.minimum(j, nk - 1))),
            pl.BlockSpec((d, tn), lambda i, j: (0, jnp.clip(j, nk, nk + nv - 1))),
            pl.BlockSpec((d, tn), lambda i, j: (0, jnp.maximum(j - nk - nv, 0))),
            gain(), gain(), tab(), tab(),
        ],
        out_specs=pl.BlockSpec((tm, tn), lambda i, j: (i, j)),
        out_shape=jax.ShapeDtypeStruct((t, kv_w.shape[1] + q_w.shape[1]), BF16),
        scratch_shapes=[pltpu.VMEM((tm, d), BF16), pltpu.VMEM((tm, d), BF16)],
        compiler_params=_cparams(2),
        name="qkv_proj",
    )(x, shk, sck, ngk.reshape(1, d), shq, scq, ngq.reshape(1, d), wk, kv_w, wq,
      _pair_split_gain(k_norm_g, 1.0), _pair_split_gain(q_norm_g, q_scale), cos2, sin2)


def _attn_kernel(q_ref, k_ref, v_ref, lam_ref, sg_ref, o_ref, qs_ref, m_ref, l_ref, a_ref,
                 s0_ref, s1_ref, p0_ref, p1_ref, al0_ref, al1_ref, *, lam_init, tq):
    i = pl.program_id(2)
    vd = a_ref.shape[-1]
    nt = (((1,), (1,)), ((), ()))
    s_refs, p_refs, al_refs = (s0_ref, s1_ref), (p0_ref, p1_ref), (al0_ref, al1_ref)

    lane = lax.broadcasted_iota(jnp.int32, (1, vd), 1)
    is_a = (lane % HEAD_DIM) < HEAD_DIM // 2
    q = q_ref[...]
    zero = jnp.zeros_like(q)
    qs_ref[:tq] = jnp.where(is_a, q, zero)
    qs_ref[tq:] = jnp.where(is_a, zero, q)
    m_ref[...] = jnp.full(m_ref.shape, MASK_VALUE, F32)
    l_ref[...] = jnp.zeros(l_ref.shape, F32)
    a_ref[...] = jnp.zeros(a_ref.shape, F32)

    def scores(blk, slot, masked=False):
        r = pl.multiple_of(blk * tq, tq)
        s = lax.dot_general(qs_ref[...], k_ref[pl.ds(r, tq), :], nt,
                            preferred_element_type=F32)
        if masked:
            row = lax.broadcasted_iota(jnp.int32, (2 * tq, tq), 0) % tq
            col = lax.broadcasted_iota(jnp.int32, (2 * tq, tq), 1)
            s = jnp.where(col <= row, s, MASK_VALUE)
        s_refs[slot][...] = s

    def softmax(slot):
        s = s_refs[slot][...]
        m_old = m_ref[...]
        m_new = jnp.maximum(m_old, jnp.max(s, axis=-1, keepdims=True))
        alpha = jnp.exp2(m_old - m_new)
        p = jnp.exp2(s - jnp.tile(m_new, (1, tq // LANES)))
        l_ref[...] = alpha * l_ref[...] + jnp.sum(p, axis=-1, keepdims=True)
        m_ref[...] = m_new
        p_refs[slot][...] = p.astype(BF16)
        al_refs[slot][...] = alpha

    def accumulate(blk, slot):
        r = pl.multiple_of(blk * tq, tq)
        pv = jnp.dot(p_refs[slot][...], v_ref[pl.ds(r, tq), :], preferred_element_type=F32)
        a_ref[...] = jnp.tile(al_refs[slot][...], (1, vd // LANES)) * a_ref[...] + pv

    def block_at(pos):
        return jnp.where(pos == 0, i, pos - 1)

    def step(t, par):
        accumulate(block_at(t - 2), par)
        softmax(1 - par)
        scores(t - 1, par)

    scores(i, 0, masked=True)

    @pl.when(i == 0)
    def _():
        softmax(0)
        accumulate(i, 0)

    @pl.when(i >= 1)
    def _():
        softmax(0)
        scores(0, 1)

        def pair(u, carry):
            t = 2 + 2 * u
            step(t, 0)
            step(t + 1, 1)
            return carry

        lax.fori_loop(0, (i - 1) // 2, pair, 0)

        @pl.when(i % 2 == 0)
        def _():
            step(i, 0)
            accumulate(block_at(i - 1), 1)
            softmax(0)
            accumulate(block_at(i), 0)

        @pl.when(i % 2 == 1)
        def _():
            accumulate(block_at(i - 1), 0)
            softmax(1)
            accumulate(block_at(i), 1)

    lp = lam_ref[...]
    lam = (jnp.exp(jnp.sum(lp[0:1] * lp[1:2], axis=-1, keepdims=True))
           - jnp.exp(jnp.sum(lp[2:3] * lp[3:4], axis=-1, keepdims=True)) + np.float32(lam_init))
    o1 = a_ref[:tq] / jnp.tile(l_ref[:tq], (1, vd // LANES))
    o2 = a_ref[tq:] / jnp.tile(l_ref[tq:], (1, vd // LANES))
    o = o1 - lam * o2
    ms = jnp.mean(o * o, axis=-1, keepdims=True)
    o = (o * lax.rsqrt(ms + EPS)) * sg_ref[...] * np.float32(1.0 - lam_init)
    o_ref[...] = o.astype(o_ref.dtype)


def _diff_attention(qkv, lam_params, subln_g, lam_init, n_heads, tq=512):
    b, s, d3 = qkv.shape
    d = d3 // 3
    vd = 2 * HEAD_DIM
    return pl.pallas_call(
        functools.partial(_attn_kernel, lam_init=lam_init, tq=tq),
        grid=(b, n_heads, s // tq),
        in_specs=[
            pl.BlockSpec((None, tq, vd), lambda bi, h, i: (bi, i, 2 * n_heads + h)),
            pl.BlockSpec((None, s, vd), lambda bi, h, i: (bi, 0, h)),
            pl.BlockSpec((None, s, vd), lambda bi, h, i: (bi, 0, n_heads + h)),
            pl.BlockSpec((4, HEAD_DIM), lambda bi, h, i: (0, 0)),
            pl.BlockSpec((1, vd), lambda bi, h, i: (0, 0)),
        ],
        out_specs=pl.BlockSpec((None, tq, vd), lambda bi, h, i: (bi, i, h)),
        out_shape=jax.ShapeDtypeStruct((b, s, d), BF16),
        scratch_shapes=[pltpu.VMEM((2 * tq, vd), BF16),
                        pltpu.VMEM((2 * tq, LANES), F32), pltpu.VMEM((2 * tq, LANES), F32),
                        pltpu.VMEM((2 * tq, vd), F32),
                        pltpu.VMEM((2 * tq, tq), F32), pltpu.VMEM((2 * tq, tq), F32),
                        pltpu.VMEM((2 * tq, tq), BF16), pltpu.VMEM((2 * tq, tq), BF16),
                        pltpu.VMEM((2 * tq, LANES), F32), pltpu.VMEM((2 * tq, LANES), F32)],
        compiler_params=_cparams(3),
        name="diff_attention",
    )(qkv, qkv, qkv, lam_params, subln_g.reshape(1, vd))


def _oproj_kernel(a_ref, w_ref, x_ref, gate_ref, o_ref):
    y = jnp.dot(a_ref[...], w_ref[...].astype(BF16), preferred_element_type=F32)
    o_ref[...] = x_ref[...] + gate_ref[...] * y


def _oproj(a, w, x, gate, seq, tm=1024, tn=1024):
    t, d = x.shape
    k = a.shape[1]
    tpb = seq // tm
    return pl.pallas_call(
        _oproj_kernel,
        grid=(d // tn, t // tm),
        in_specs=[
            pl.BlockSpec((tm, k), lambda j, i: (i, 0)),
            pl.BlockSpec((k, tn), lambda j, i: (0, j)),
            pl.BlockSpec((tm, tn), lambda j, i: (i, j)),
            pl.BlockSpec((None, 1, tn), lambda j, i: (i // tpb, 0, j)),
        ],
        out_specs=pl.BlockSpec((tm, tn), lambda j, i: (i, j)),
        out_shape=jax.ShapeDtypeStruct((t, d), F32),
        compiler_params=_cparams(2),
        name="out_proj",
    )(a, w, x, gate)


def kernel(x, c, positions, a_ada_w, a_ada_b, a_norm1_g, a_in_w, a_in_b, a_sgu_ln_g, a_sgu_ln_b, a_sgu_w, a_sgu_b, a_out_w, a_norm2_g, a_ffn_wi, a_ffn_wo, kv_ada_w, kv_ada_b, kv_norm_g, kv_w, k_norm_g, b_ada_w, b_ada_b, b_norm1_g, b_q_w, b_q_norm_g, b_lambda_q1, b_lambda_k1, b_lambda_q2, b_lambda_k2, b_subln_g, b_o_w, b_norm2_g, b_ffn_wi, b_ffn_wo):
    bsz, seq, d = x.shape
    n_a = a_ada_w.shape[0]
    n_b = b_ada_w.shape[0]
    n_heads = d // (2 * HEAD_DIM)
    t = bsz * seq
    q_scale = HEAD_DIM ** -0.5 * math.log2(math.e)

    cos2, sin2 = _rope_tables(positions)
    xf = x.reshape(t, d)

    for l in range(n_a):
        sh1, sc1, g1, sh2, sc2, g2 = _ada_mod(c, a_ada_w[l], a_ada_b[l], 6)
        xf = _gmlp_mix(xf, sh1, sc1, g1, a_norm1_g[l], a_in_w[l], a_in_b[l], a_sgu_ln_g[l],
                       a_sgu_ln_b[l], a_sgu_w[l], a_sgu_b[l], a_out_w[l], seq)
        xf = _ffn(xf, sh2, sc2, g2, a_norm2_g[l], a_ffn_wi[l], a_ffn_wo[l], seq)

    assert n_b == 1, "only one differential-attention layer is supported"
    shk, sck = _ada_mod(c, kv_ada_w, kv_ada_b, 2)

    for jl in range(n_b):
        layer_idx = n_a + jl
        lam_init = 0.8 - 0.6 * math.exp(-0.3 * layer_idx)
        sh1, sc1, g1, sh2, sc2, g2 = _ada_mod(c, b_ada_w[jl], b_ada_b[jl], 6)
        qkv = _qkv_proj(xf, shk, sck, kv_norm_g, sh1, sc1, b_norm1_g[jl], kv_w, b_q_w[jl],
                        k_norm_g, b_q_norm_g[jl], cos2, sin2, seq, q_scale)
        lam_params = jnp.stack([b_lambda_q1[jl], b_lambda_k1[jl],
                                b_lambda_q2[jl], b_lambda_k2[jl]])
        o = _diff_attention(qkv.reshape(bsz, seq, 3 * d), lam_params, b_subln_g[jl], lam_init,
                            n_heads)
        xf = _oproj(o.reshape(t, d), b_o_w[jl], xf, g1, seq)
        xf = _ffn(xf, sh2, sc2, g2, b_norm2_g[jl], b_ffn_wi[jl], b_ffn_wo[jl], seq)

    return xf.reshape(bsz, seq, d)
```

```python
import functools
import math

import jax
import jax.numpy as jnp
import numpy as np
from jax import lax
from jax.experimental import pallas as pl
from jax.experimental.pallas import tpu as pltpu

F32 = jnp.float32
BF16 = jnp.bfloat16

EPS = 1e-6
ROPE_THETA = 10000.0
HEAD_DIM = 128
CHUNK = 128
LANES = 128
SUBLANES = 8
MXU_COLS = 256
VMEM_LIMIT_BYTES = 58 * 1024 * 1024
MASK_VALUE = -1e30
NORM_ROWS = 128
ROW_PIECES = 4


def _cparams(n_axes):
    return pltpu.CompilerParams(
        dimension_semantics=("arbitrary",) * n_axes,
        vmem_limit_bytes=VMEM_LIMIT_BYTES,
    )


def _sigmoid(x):
    return 1.0 / (1.0 + jnp.exp(-x))


def _norm_mod_rows(x_ref, g_ref, sh_ref, sc_ref, h_ref, r0, n_rows):
    gain = g_ref[...] * (1.0 + sc_ref[...])
    shift = sh_ref[...]
    for c in range(n_rows // NORM_ROWS):
        rows = pl.ds(r0 + c * NORM_ROWS, NORM_ROWS)
        x = x_ref[rows, :]
        y = x * lax.rsqrt(jnp.mean(x * x, axis=-1, keepdims=True) + EPS)
        h_ref[rows, :] = (y * gain + shift).astype(BF16)


def _rope_kernel(pos_ref, freq_ref, cos_ref, sin_ref):
    ang = pos_ref[...].astype(F32) * freq_ref[...]
    cos_ref[...] = jnp.cos(ang)
    sin_ref[...] = jnp.sin(ang)


def _rope_tables(positions, tm=1024):
    t = positions.size
    inv_freq = 1.0 / (ROPE_THETA ** (jnp.arange(0, HEAD_DIM, 2, dtype=F32) / HEAD_DIM))
    tab = jnp.concatenate([inv_freq, inv_freq]).reshape(1, HEAD_DIM)
    pos = positions.reshape(t, 1)
    return pl.pallas_call(
        _rope_kernel,
        grid=(t // tm,),
        in_specs=[pl.BlockSpec((tm, 1), lambda i: (i, 0)),
                  pl.BlockSpec((1, HEAD_DIM), lambda i: (0, 0))],
        out_specs=[pl.BlockSpec((tm, HEAD_DIM), lambda i: (i, 0)),
                   pl.BlockSpec((tm, HEAD_DIM), lambda i: (i, 0))],
        out_shape=[jax.ShapeDtypeStruct((t, HEAD_DIM), F32)] * 2,
        compiler_params=_cparams(1),
        name="rope_tables",
    )(pos, tab)


def _ada_kernel(c_ref, w_ref, b_ref, o_ref, s_ref, *, row_chunk):
    nb, k, _ = c_ref.shape
    tn = w_ref.shape[1]
    n_chunks = k // row_chunk

    @pl.when(pl.program_id(0) == 0)
    def _():
        for b in range(nb):
            cb = c_ref[b]
            s_ref[b] = jnp.broadcast_to(cb * _sigmoid(cb), (k, LANES))

    def body(kk, accs):
        r = pl.multiple_of(kk * row_chunk, row_chunk)
        w = w_ref[pl.ds(r, row_chunk), :]
        new = []
        for b in range(nb):
            sb = jnp.tile(s_ref[b, pl.ds(r, row_chunk), :], (1, tn // LANES))
            prod = (w * sb).reshape(row_chunk // SUBLANES, SUBLANES, tn)
            new.append(accs[b] + jnp.sum(prod, axis=0))
        return tuple(new)

    accs = lax.fori_loop(0, n_chunks, body,
                         tuple(jnp.zeros((SUBLANES, tn), F32) for _ in range(nb)))
    for b in range(nb):
        o_ref[b:b + 1, :] = jnp.sum(accs[b], axis=0, keepdims=True) + b_ref[...]


def _ada_mod(c, w, b, n, tn=1024, row_chunk=64):
    nb, k = c.shape
    nd = w.shape[1]
    out = pl.pallas_call(
        functools.partial(_ada_kernel, row_chunk=row_chunk),
        grid=(nd // tn,),
        in_specs=[pl.BlockSpec((nb, k, 1), lambda j: (0, 0, 0)),
                  pl.BlockSpec((k, tn), lambda j: (0, j)),
                  pl.BlockSpec((1, tn), lambda j: (0, j))],
        out_specs=pl.BlockSpec((nb, tn), lambda j: (0, j)),
        out_shape=jax.ShapeDtypeStruct((nb, nd), F32),
        scratch_shapes=[pltpu.VMEM((nb, k, LANES), F32)],
        compiler_params=_cparams(1),
        name="ada_mod",
    )(c.reshape(nb, k, 1), w, b.reshape(1, nd))
    d = nd // n
    return [out[:, i * d:(i + 1) * d].reshape(nb, 1, d) for i in range(n)]


def _gelu(z):
    return 0.5 * z * (1.0 + lax.erf(z * np.float32(math.sqrt(0.5))))


def _gmlp_kernel(x_ref, sh_ref, sc_ref, gate_ref, ng_ref, inw_ref, inb_ref, lng_ref, lnb_ref,
                 sguw_ref, sgub_ref, outw_ref, o_ref, h_ref, sv_ref, p_ref, wm_ref,
                 *, nv, tn1, tn2):
    j = pl.program_id(1)
    tm, d = x_ref.shape
    n_groups = sguw_ref.shape[0]

    tr = tm // ROW_PIECES
    norm = (ng_ref, sh_ref, sc_ref, h_ref)

    def in_proj(rp):
        rows = slice(rp * tr, (rp + 1) * tr)
        z = jnp.dot(h_ref[rows, :], inw_ref[...].astype(BF16), preferred_element_type=F32)
        return rows, _gelu(z + inb_ref[...])

    def sgu_rows(r0):
        rows = slice(r0, r0 + CHUNK)
        v = sv_ref[rows, :]
        mu = jnp.mean(v, axis=-1, keepdims=True)
        vc = v - mu
        var = jnp.mean(vc * vc, axis=-1, keepdims=True)
        vn = ((vc * lax.rsqrt(var + EPS)) * lng_ref[...] + lnb_ref[...]).astype(BF16)
        for g in range(n_groups):
            lo = g * CHUNK
            sv = jnp.dot(wm_ref[g], vn[:, lo:lo + CHUNK], preferred_element_type=F32)
            sv_ref[rows, lo:lo + CHUNK] = sv + sgub_ref[:, lo:lo + CHUNK]

    @pl.when(j == 0)
    def _():
        for rp in range(ROW_PIECES):
            _norm_mod_rows(x_ref, *norm, rp * tr, tr)
            rows, z = in_proj(rp)
            sv_ref[rows, :tn1] = z

    @pl.when((j > 0) & (j < nv))
    def _():
        c0 = pl.multiple_of(j * tn1, tn1)
        for rp in range(ROW_PIECES):
            rows, z = in_proj(rp)
            sv_ref[rows, pl.ds(c0, tn1)] = z

    @pl.when(j == nv)
    def _():
        row = lax.broadcasted_iota(jnp.int32, (CHUNK, CHUNK), 0)
        col = lax.broadcasted_iota(jnp.int32, (CHUNK, CHUNK), 1)
        for g in range(n_groups):
            wm_ref[g] = jnp.where(col <= row, sguw_ref[g], 0.0).astype(BF16)
        for rp in range(ROW_PIECES):
            rows, z = in_proj(rp)
            for r0 in range(rp * tr, (rp + 1) * tr, CHUNK):
                sgu_rows(r0)
            p_ref[rows, :tn1] = (z * sv_ref[rows, :tn1]).astype(BF16)

    @pl.when((j > nv) & (j < 2 * nv))
    def _():
        c0 = pl.multiple_of((j - nv) * tn1, tn1)
        for rp in range(ROW_PIECES):
            rows, z = in_proj(rp)
            p_ref[rows, pl.ds(c0, tn1)] = (z * sv_ref[rows, pl.ds(c0, tn1)]).astype(BF16)

    @pl.when(j >= 2 * nv)
    def _():
        y = jnp.dot(p_ref[...], outw_ref[...].astype(BF16), preferred_element_type=F32)
        c0 = pl.multiple_of((j - 2 * nv) * tn2, tn2)
        o_ref[...] = x_ref[:, pl.ds(c0, tn2)] + gate_ref[:, pl.ds(c0, tn2)] * y


def _gmlp_mix(x, sh, sc, gate, norm_g, in_w, in_b, ln_g, ln_b, sgu_w, sgu_b, out_w, seq,
              tm=1024, tn1=1024, tn2=512):
    t, d = x.shape
    nv = d // tn1
    n2 = d // tn2
    tpb = seq // tm
    n_groups = sgu_w.shape[0]
    sgub_full = jnp.repeat(sgu_b.T, d // n_groups, axis=1)
    vec = lambda: pl.BlockSpec((None, 1, d), lambda i, j: (i // tpb, 0, 0))
    row = lambda n: pl.BlockSpec((1, n), lambda i, j: (0, 0))

    def in_col(i, j):
        jj = jnp.minimum(j, 2 * nv - 1)
        return (0, jnp.where(jj < nv, jj + nv, jj - nv))

    out_col = lambda i, j: jnp.maximum(j - 2 * nv, 0)
    return pl.pallas_call(
        functools.partial(_gmlp_kernel, nv=nv, tn1=tn1, tn2=tn2),
        grid=(t // tm, 2 * nv + n2),
        in_specs=[
            pl.BlockSpec((tm, d), lambda i, j: (i, 0)),
            vec(), vec(), vec(), row(d),
            pl.BlockSpec((d, tn1), in_col),
            pl.BlockSpec((1, tn1), in_col),
            row(d), row(d),
            pl.BlockSpec((n_groups, CHUNK, CHUNK), lambda i, j: (0, 0, 0)),
            pl.BlockSpec((CHUNK, d), lambda i, j: (0, 0)),
            pl.BlockSpec((d, tn2), lambda i, j: (0, out_col(i, j))),
        ],
        out_specs=pl.BlockSpec((tm, tn2), lambda i, j: (i, out_col(i, j))),
        out_shape=jax.ShapeDtypeStruct((t, d), F32),
        scratch_shapes=[pltpu.VMEM((tm, d), BF16),
                        pltpu.VMEM((tm, d), F32),
                        pltpu.VMEM((tm, d), BF16),
                        pltpu.VMEM((n_groups, CHUNK, CHUNK), BF16)],
        compiler_params=_cparams(2),
        name="gmlp_mix",
    )(x, sh, sc, gate, norm_g.reshape(1, d), in_w.astype(BF16), in_b.reshape(1, 2 * d),
      ln_g.reshape(1, d), ln_b.reshape(1, d), sgu_w, sgub_full, out_w.astype(BF16))


def _ffn_kernel(x_ref, sh_ref, sc_ref, gate_ref, ng_ref, wg_ref, wu_ref, wo_ref, o_ref, h_ref):
    j = pl.program_id(1)
    tm = x_ref.shape[0]
    tr = tm // ROW_PIECES

    def gated_down(rows):
        h = h_ref[rows, :]
        g = jnp.dot(h, wg_ref[...].astype(BF16), preferred_element_type=F32)
        u = jnp.dot(h, wu_ref[...].astype(BF16), preferred_element_type=F32)
        a = (g * _sigmoid(g) * u).astype(BF16)
        y = jnp.dot(a, wo_ref[...].astype(BF16), preferred_element_type=F32)
        return gate_ref[...] * y

    @pl.when(j == 0)
    def _():
        for rp in range(ROW_PIECES):
            rows = slice(rp * tr, (rp + 1) * tr)
            _norm_mod_rows(x_ref, ng_ref, sh_ref, sc_ref, h_ref, rp * tr, tr)
            o_ref[rows, :] = x_ref[rows, :] + gated_down(rows)

    @pl.when(j > 0)
    def _():
        o_ref[...] += gated_down(slice(None))


def _ffn(x, sh, sc, gate, norm_g, wi, wo, seq, tm=1024, tf=512):
    t, d = x.shape
    f = wo.shape[0]
    nf = f // tf
    tpb = seq // tm
    vec = lambda: pl.BlockSpec((None, 1, d), lambda i, j: (i // tpb, 0, 0))
    return pl.pallas_call(
        _ffn_kernel,
        grid=(t // tm, nf),
        in_specs=[
            pl.BlockSpec((tm, d), lambda i, j: (i, 0), pipeline_mode=pl.Buffered(1)),
            vec(), vec(), vec(),
            pl.BlockSpec((1, d), lambda i, j: (0, 0)),
            pl.BlockSpec((d, tf), lambda i, j: (0, j)),
            pl.BlockSpec((d, tf), lambda i, j: (0, j + nf)),
            pl.BlockSpec((tf, d), lambda i, j: (j, 0)),
        ],
        out_specs=pl.BlockSpec((tm, d), lambda i, j: (i, 0)),
        out_shape=jax.ShapeDtypeStruct((t, d), F32),
        scratch_shapes=[pltpu.VMEM((tm, d), BF16)],
        compiler_params=_cparams(2),
        name="swiglu_ffn",
    )(x, sh, sc, gate, norm_g.reshape(1, d), wi, wi, wo)


def _pair_split_kernel(w_ref, o_ref):
    half = HEAD_DIM // 2
    first = lax.broadcasted_iota(jnp.int32, (1, HEAD_DIM), 1) < half
    for pr in range(w_ref.shape[1] // MXU_COLS):
        c0 = pr * MXU_COLS
        a = w_ref[:, c0:c0 + HEAD_DIM]
        b = w_ref[:, c0 + HEAD_DIM:c0 + MXU_COLS]
        lo = jnp.where(first, a, pltpu.roll(b, half, axis=1))
        hi = jnp.where(first, pltpu.roll(a, half, axis=1), b)
        o_ref[:, c0:c0 + HEAD_DIM] = lo.astype(o_ref.dtype)
        o_ref[:, c0 + HEAD_DIM:c0 + MXU_COLS] = hi.astype(o_ref.dtype)


def _pair_split_cols(w, n, tr=256):
    d = w.shape[0]
    return pl.pallas_call(
        _pair_split_kernel,
        grid=(d // tr,),
        in_specs=[pl.BlockSpec((tr, n), lambda i: (i, 0))],
        out_specs=pl.BlockSpec((tr, n), lambda i: (i, 0)),
        out_shape=jax.ShapeDtypeStruct((d, n), BF16),
        compiler_params=_cparams(1),
        name="pair_split_cols",
    )(w)


def _pair_split_gain(g, scale):
    half = HEAD_DIM // 2
    g = g * np.float32(scale)
    return jnp.stack([jnp.tile(g[:half], 2), jnp.tile(g[half:], 2)])


def _qkv_kernel(x_ref, shk_ref, sck_ref, ngk_ref, shq_ref, scq_ref, ngq_ref, wk_ref, wv_ref,
                wq_ref, hgk_ref, hgq_ref, cos_ref, sin_ref, o_ref, hkv_ref, hq_ref,
                *, nk, nv):
    j = pl.program_id(1)
    tn = o_ref.shape[1]

    tm = o_ref.shape[0]
    tr = tm // ROW_PIECES
    kv_norm = (ngk_ref, shk_ref, sck_ref, hkv_ref)
    q_norm = (ngq_ref, shq_ref, scq_ref, hq_ref)

    def project_rope(h_ref, w_ref, hg_ref, before_rows=None):
        g_lo = hg_ref[0:1, :]
        g_hi = hg_ref[1:2, :]
        is_a = lax.broadcasted_iota(jnp.int32, (1, HEAD_DIM), 1) < HEAD_DIM // 2
        inv_n = np.float32(1.0 / HEAD_DIM)
        for rp in range(ROW_PIECES):
            if before_rows is not None:
                before_rows(rp * tr, tr)
            rows = slice(rp * tr, (rp + 1) * tr)
            cos = cos_ref[rows, :]
            sin = sin_ref[rows, :]
            t_ll, t_lh = cos * g_lo, -(sin * g_hi)
            t_hh, t_hl = cos * g_hi, sin * g_lo
            for c in range(tn // MXU_COLS):
                c0 = c * MXU_COLS
                y = jnp.dot(h_ref[rows, :], w_ref[:, c0:c0 + MXU_COLS],
                            preferred_element_type=F32)
                y_lo = y[:, :HEAD_DIM]
                y_hi = y[:, HEAD_DIM:]
                sq = y_lo * y_lo + y_hi * y_hi
                sum_a = jnp.sum(jnp.where(is_a, sq, 0.0), axis=-1, keepdims=True)
                sum_b = jnp.sum(sq, axis=-1, keepdims=True) - sum_a
                rs = jnp.where(is_a, lax.rsqrt(sum_a * inv_n + EPS),
                               lax.rsqrt(sum_b * inv_n + EPS))
                o_ref[rows, c0:c0 + HEAD_DIM] = (
                    (y_lo * t_ll + y_hi * t_lh) * rs).astype(o_ref.dtype)
                o_ref[rows, c0 + HEAD_DIM:c0 + MXU_COLS] = (
                    (y_hi * t_hh + y_lo * t_hl) * rs).astype(o_ref.dtype)

    @pl.when(j == 0)
    def _():
        project_rope(hkv_ref, wk_ref, hgk_ref,
                     before_rows=lambda r0, n: _norm_mod_rows(x_ref, *kv_norm, r0, n))

    @pl.when((j > 0) & (j < nk))
    def _():
        project_rope(hkv_ref, wk_ref, hgk_ref)

    @pl.when((j >= nk) & (j < nk + nv))
    def _():
        q_rows = tm // nv
        _norm_mod_rows(x_ref, *q_norm, pl.multiple_of((j - nk) * q_rows, q_rows), q_rows)
        for c in range(tn // MXU_COLS):
            c0 = c * MXU_COLS
            y = jnp.dot(hkv_ref[...], wv_ref[:, c0:c0 + MXU_COLS].astype(BF16),
                        preferred_element_type=F32)
            o_ref[:, c0:c0 + MXU_COLS] = y.astype(o_ref.dtype)

    @pl.when(j >= nk + nv)
    def _():
        project_rope(hq_ref, wq_ref, hgq_ref)


def _qkv_proj(x, shk, sck, ngk, shq, scq, ngq, kv_w, q_w, k_norm_g, q_norm_g, cos2, sin2, seq,
              q_scale, tm=1024, tn=512):
    t, d = x.shape
    nk = d // tn
    nv = (kv_w.shape[1] - d) // tn
    nq = q_w.shape[1] // tn
    tpb = seq // tm
    wk = _pair_split_cols(kv_w, d)
    wq = _pair_split_cols(q_w, q_w.shape[1])
    vec = lambda: pl.BlockSpec((None, 1, d), lambda i, j: (i // tpb, 0, 0))
    row = lambda n: pl.BlockSpec((1, n), lambda i, j: (0, 0))
    gain = lambda: pl.BlockSpec((2, HEAD_DIM), lambda i, j: (0, 0))
    tab = lambda: pl.BlockSpec((tm, HEAD_DIM), lambda i, j: (i, 0))
    return pl.pallas_call(
        functools.partial(_qkv_kernel, nk=nk, nv=nv),
        grid=(t // tm, nk + nv + nq),
        in_specs=[
            pl.BlockSpec((tm, d), lambda i, j: (i, 0)),
            vec(), vec(), row(d), vec(), vec(), row(d),
            pl.BlockSpec((d, tn), lambda i, j: (0, jnp.minimum(j, nk - 1))),
            pl.BlockSpec((d, tn), lambda i, j: (0, jnp.clip(j, nk, nk + nv - 1))),
            pl.BlockSpec((d, tn), lambda i, j: (0, jnp.maximum(j - nk - nv, 0))),
            gain(), gain(), tab(), tab(),
        ],
        out_specs=pl.BlockSpec((tm, tn), lambda i, j: (i, j)),
        out_shape=jax.ShapeDtypeStruct((t, kv_w.shape[1] + q_w.shape[1]), BF16),
        scratch_shapes=[pltpu.VMEM((tm, d), BF16), pltpu.VMEM((tm, d), BF16)],
        compiler_params=_cparams(2),
        name="qkv_proj",
    )(x, shk, sck, ngk.reshape(1, d), shq, scq, ngq.reshape(1, d), wk, kv_w, wq,
      _pair_split_gain(k_norm_g, 1.0), _pair_split_gain(q_norm_g, q_scale), cos2, sin2)


def _attn_kernel(q_ref, k_ref, v_ref, lam_ref, sg_ref, o_ref, qs_ref, m_ref, l_ref, a_ref,
                 s0_ref, s1_ref, p0_ref, p1_ref, al0_ref, al1_ref, *, lam_init, tq):
    i = pl.program_id(2)
    vd = a_ref.shape[-1]
    nt = (((1,), (1,)), ((), ()))
    s_refs, p_refs, al_refs = (s0_ref, s1_ref), (p0_ref, p1_ref), (al0_ref, al1_ref)

    lane = lax.broadcasted_iota(jnp.int32, (1, vd), 1)
    is_a = (lane % HEAD_DIM) < HEAD_DIM // 2
    q = q_ref[...]
    zero = jnp.zeros_like(q)
    qs_ref[:tq] = jnp.where(is_a, q, zero)
    qs_ref[tq:] = jnp.where(is_a, zero, q)
    m_ref[...] = jnp.full(m_ref.shape, MASK_VALUE, F32)
    l_ref[...] = jnp.zeros(l_ref.shape, F32)
    a_ref[...] = jnp.zeros(a_ref.shape, F32)

    def scores(blk, slot, masked=False):
        r = pl.multiple_of(blk * tq, tq)
        s = lax.dot_general(qs_ref[...], k_ref[pl.ds(r, tq), :], nt,
                            preferred_element_type=F32)
        if masked:
            row = lax.broadcasted_iota(jnp.int32, (2 * tq, tq), 0) % tq
            col = lax.broadcasted_iota(jnp.int32, (2 * tq, tq), 1)
            s = jnp.where(col <= row, s, MASK_VALUE)
        s_refs[slot][...] = s

    def softmax(slot):
        s = s_refs[slot][...]
        m_old = m_ref[...]
        m_new = jnp.maximum(m_old, jnp.max(s, axis=-1, keepdims=True))
        alpha = jnp.exp2(m_old - m_new)
        p = jnp.exp2(s - jnp.tile(m_new, (1, tq // LANES)))
        l_ref[...] = alpha * l_ref[...] + jnp.sum(p, axis=-1, keepdims=True)
        m_ref[...] = m_new
        p_refs[slot][...] = p.astype(BF16)
        al_refs[slot][...] = alpha

    def accumulate(blk, slot):
        r = pl.multiple_of(blk * tq, tq)
        pv = jnp.dot(p_refs[slot][...], v_ref[pl.ds(r, tq), :], preferred_element_type=F32)
        a_ref[...] = jnp.tile(al_refs[slot][...], (1, vd // LANES)) * a_ref[...] + pv

    def block_at(pos):
        return jnp.where(pos == 0, i, pos - 1)

    def step(t, par):
        accumulate(block_at(t - 2), par)
        softmax(1 - par)
        scores(t - 1, par)

    scores(i, 0, masked=True)

    @pl.when(i == 0)
    def _():
        softmax(0)
        accumulate(i, 0)

    @pl.when(i >= 1)
    def _():
        softmax(0)
        scores(0, 1)

        def pair(u, carry):
            t = 2 + 2 * u
            step(t, 0)
            step(t + 1, 1)
            return carry

        lax.fori_loop(0, (i - 1) // 2, pair, 0)

        @pl.when(i % 2 == 0)
        def _():
            step(i, 0)
            accumulate(block_at(i - 1), 1)
            softmax(0)
            accumulate(block_at(i), 0)

        @pl.when(i % 2 == 1)
        def _():
            accumulate(block_at(i - 1), 0)
            softmax(1)
            accumulate(block_at(i), 1)

    lp = lam_ref[...]
    lam = (jnp.exp(jnp.sum(lp[0:1] * lp[1:2], axis=-1, keepdims=True))
           - jnp.exp(jnp.sum(lp[2:3] * lp[3:4], axis=-1, keepdims=True)) + np.float32(lam_init))
    o1 = a_ref[:tq] / jnp.tile(l_ref[:tq], (1, vd // LANES))
    o2 = a_ref[tq:] / jnp.tile(l_ref[tq:], (1, vd // LANES))
    o = o1 - lam * o2
    ms = jnp.mean(o * o, axis=-1, keepdims=True)
    o = (o * lax.rsqrt(ms + EPS)) * sg_ref[...] * np.float32(1.0 - lam_init)
    o_ref[...] = o.astype(o_ref.dtype)


def _diff_attention(qkv, lam_params, subln_g, lam_init, n_heads, tq=512):
    b, s, d3 = qkv.shape
    d = d3 // 3
    vd = 2 * HEAD_DIM
    return pl.pallas_call(
        functools.partial(_attn_kernel, lam_init=lam_init, tq=tq),
        grid=(b, n_heads, s // tq),
        in_specs=[
            pl.BlockSpec((None, tq, vd), lambda bi, h, i: (bi, i, 2 * n_heads + h)),
            pl.BlockSpec((None, s, vd), lambda bi, h, i: (bi, 0, h)),
            pl.BlockSpec((None, s, vd), lambda bi, h, i: (bi, 0, n_heads + h)),
            pl.BlockSpec((4, HEAD_DIM), lambda bi, h, i: (0, 0)),
            pl.BlockSpec((1, vd), lambda bi, h, i: (0, 0)),
        ],
        out_specs=pl.BlockSpec((None, tq, vd), lambda bi, h, i: (bi, i, h)),
        out_shape=jax.ShapeDtypeStruct((b, s, d), BF16),
        scratch_shapes=[pltpu.VMEM((2 * tq, vd), BF16),
                        pltpu.VMEM((2 * tq, LANES), F32), pltpu.VMEM((2 * tq, LANES), F32),
                        pltpu.VMEM((2 * tq, vd), F32),
                        pltpu.VMEM((2 * tq, tq), F32), pltpu.VMEM((2 * tq, tq), F32),
                        pltpu.VMEM((2 * tq, tq), BF16), pltpu.VMEM((2 * tq, tq), BF16),
                        pltpu.VMEM((2 * tq, LANES), F32), pltpu.VMEM((2 * tq, LANES), F32)],
        compiler_params=_cparams(3),
        name="diff_attention",
    )(qkv, qkv, qkv, lam_params, subln_g.reshape(1, vd))


def _oproj_kernel(a_ref, w_ref, x_ref, gate_ref, o_ref):
    y = jnp.dot(a_ref[...], w_ref[...].astype(BF16), preferred_element_type=F32)
    o_ref[...] = x_ref[...] + gate_ref[...] * y


def _oproj(a, w, x, gate, seq, tm=1024, tn=1024):
    t, d = x.shape
    k = a.shape[1]
    tpb = seq // tm
    return pl.pallas_call(
        _oproj_kernel,
        grid=(d // tn, t // tm),
        in_specs=[
            pl.BlockSpec((tm, k), lambda j, i: (i, 0)),
            pl.BlockSpec((k, tn), lambda j, i: (0, j)),
            pl.BlockSpec((tm, tn), lambda j, i: (i, j)),
            pl.BlockSpec((None, 1, tn), lambda j, i: (i // tpb, 0, j)),
        ],
        out_specs=pl.BlockSpec((tm, tn), lambda j, i: (i, j)),
        out_shape=jax.ShapeDtypeStruct((t, d), F32),
        compiler_params=_cparams(2),
        name="out_proj",
    )(a, w, x, gate)


def kernel(x, c, positions, a_ada_w, a_ada_b, a_norm1_g, a_in_w, a_in_b, a_sgu_ln_g, a_sgu_ln_b, a_sgu_w, a_sgu_b, a_out_w, a_norm2_g, a_ffn_wi, a_ffn_wo, kv_ada_w, kv_ada_b, kv_norm_g, kv_w, k_norm_g, b_ada_w, b_ada_b, b_norm1_g, b_q_w, b_q_norm_g, b_lambda_q1, b_lambda_k1, b_lambda_q2, b_lambda_k2, b_subln_g, b_o_w, b_norm2_g, b_ffn_wi, b_ffn_wo):
    bsz, seq, d = x.shape
    n_a = a_ada_w.shape[0]
    n_b = b_ada_w.shape[0]
    n_heads = d // (2 * HEAD_DIM)
    t = bsz * seq
    q_scale = HEAD_DIM ** -0.5 * math.log2(math.e)

    cos2, sin2 = _rope_tables(positions)
    xf = x.reshape(t, d)

    for l in range(n_a):
        sh1, sc1, g1, sh2, sc2, g2 = _ada_mod(c, a_ada_w[l], a_ada_b[l], 6)
        xf = _gmlp_mix(xf, sh1, sc1, g1, a_norm1_g[l], a_in_w[l], a_in_b[l], a_sgu_ln_g[l],
                       a_sgu_ln_b[l], a_sgu_w[l], a_sgu_b[l], a_out_w[l], seq)
        xf = _ffn(xf, sh2, sc2, g2, a_norm2_g[l], a_ffn_wi[l], a_ffn_wo[l], seq)

    assert n_b == 1, "only one differential-attention layer is supported"
    shk, sck = _ada_mod(c, kv_ada_w, kv_ada_b, 2)

    for jl in range(n_b):
        layer_idx = n_a + jl
        lam_init = 0.8 - 0.6 * math.exp(-0.3 * layer_idx)
        sh1, sc1, g1, sh2, sc2, g2 = _ada_mod(c, b_ada_w[jl], b_ada_b[jl], 6)
        qkv = _qkv_proj(xf, shk, sck, kv_norm_g, sh1, sc1, b_norm1_g[jl], kv_w, b_q_w[jl],
                        k_norm_g, b_q_norm_g[jl], cos2, sin2, seq, q_scale)
        lam_params = jnp.stack([b_lambda_q1[jl], b_lambda_k1[jl],
                                b_lambda_q2[jl], b_lambda_k2[jl]])
        o = _diff_attention(qkv.reshape(bsz, seq, 3 * d), lam_params, b_subln_g[jl], lam_init,
                            n_heads)
        xf = _oproj(o.reshape(t, d), b_o_w[jl], xf, g1, seq)
        xf = _ffn(xf, sh2, sc2, g2, b_norm2_g[jl], b_ffn_wi[jl], b_ffn_wo[jl], seq)

    return xf.reshape(bsz, seq, d)
```
